```python
import jax, jax.numpy as jnp
from jax import lax
import numpy as np

D_MODEL = 1024
BATCH = 32
SEQ = 256
DEPTH = 4
DEC_BATCH = 4
DEC_SEQ = 2048
PAST_LEN = 512

GRID_W = 64
N_MIXERS = 2
N_CONV_LAYERS = (DEPTH + 1) // 2
N_DN_LAYERS = DEPTH // 2
D_FF = 2816
CONV_WIDTH = 31
DN_HEADS = 8
DN_DK = 128
DN_DV = 128
DN_QK = DN_HEADS * DN_DK
DN_VD = DN_HEADS * DN_DV
DN_PROJ = 2 * DN_QK + 2 * DN_VD + 4 * DN_HEADS
DN_SHORT_CONV = 3
DN_CHUNK = 64
N_MOD = 9
DEEPNORM_ALPHA = (2.0 * DEPTH) ** 0.25
DEEPNORM_BETA = (8.0 * DEPTH) ** -0.25
LN_EPS = 1e-5

kernel_name = "hybrid_conv_deltanet_flow_step"


def layer_norm(x, g, b):
    xf = x.astype(jnp.float32)
    xc = xf - jnp.mean(xf, -1, keepdims=True)
    var = jnp.mean(xc * xc, -1, keepdims=True)
    y = xc * lax.rsqrt(var + LN_EPS) * g.astype(jnp.float32) + b.astype(jnp.float32)
    return y.astype(x.dtype)


def rms_norm(x, g):
    xf = x.astype(jnp.float32)
    return xf * lax.rsqrt(jnp.mean(xf * xf, -1, keepdims=True) + LN_EPS) * g.astype(jnp.float32)


def l2_normalize(x):
    return x * lax.rsqrt(jnp.sum(x * x, -1, keepdims=True) + 1e-6)


def adaln_params(cond, w_mod_l, b_mod_l):
    m = jax.nn.silu(cond) @ w_mod_l + b_mod_l
    return m.reshape(cond.shape[0], 1, N_MOD, D_MODEL)


def modulate(x, m, slot):
    return x * (1.0 + m[:, :, 3 * slot + 1]) + m[:, :, 3 * slot]


def swiglu_ffn(h, w_in, w_out):
    a = h @ w_in
    return (jax.nn.silu(a[..., D_FF:]) * a[..., :D_FF]) @ w_out


def dwconv_seq(x, w):
    k, ch = w.shape
    return lax.conv_general_dilated(x, w.astype(x.dtype)[:, None, :], (1,), [(k // 2, k // 2)],
                                    dimension_numbers=("NWC", "WIO", "NWC"), feature_group_count=ch)


def dwconv_grid(x, w):
    bsz, length, ch = x.shape
    rows = length // GRID_W
    k = w.shape[0]
    half = ch // 2
    xg = x.reshape(bsz, rows, GRID_W, ch)
    w = w.astype(x.dtype)
    dn = ("NHWC", "HWIO", "NHWC")
    horiz = lax.conv_general_dilated(xg[..., :half], w[:, :half].reshape(1, k, 1, half), (1, 1),
                                     [(0, 0), (k // 2, k // 2)], dimension_numbers=dn, feature_group_count=half)
    vert = lax.conv_general_dilated(xg[..., half:], w[:, half:].reshape(k, 1, 1, ch - half), (1, 1),
                                    [(k // 2, k // 2), (0, 0)], dimension_numbers=dn, feature_group_count=ch - half)
    return jnp.concatenate([horiz, vert], -1).reshape(bsz, length, ch)


def conv_module(h, w1, b1, dw, dwb, lng, lnb, w2, b2, on_grid):
    a = h @ w1 + b1
    a = a[..., :D_MODEL] * jax.nn.sigmoid(a[..., D_MODEL:])
    a = (dwconv_grid(a, dw) if on_grid else dwconv_seq(a, dw)) + dwb
    a = jax.nn.silu(layer_norm(a, lng, lnb))
    return a @ w2 + b2


def delta_rule_chunked(q, k, v, log_a, beta, s0):
    bsz, nh, length, dk = q.shape
    dv = v.shape[-1]
    n = length // DN_CHUNK
    r = lambda t: t.reshape(bsz, nh, n, DN_CHUNK, *t.shape[3:])
    q, k, v, log_a, beta = r(q), r(k), r(v), r(log_a), r(beta)
    g = jnp.cumsum(log_a, axis=-1)
    diff = g[..., :, None] - g[..., None, :]
    idx = jnp.arange(DN_CHUNK)
    incl = idx[:, None] >= idx[None, :]
    strict = idx[:, None] > idx[None, :]
    dec_incl = jnp.exp(jnp.where(incl, diff, -jnp.inf))
    dec_strict = jnp.where(strict, dec_incl, 0.0)
    a_mat = beta[..., :, None] * jnp.einsum("bhnid,bhnjd->bhnij", k, k) * dec_strict
    eye = jnp.eye(DN_CHUNK, dtype=jnp.float32)
    rhs = jnp.concatenate([beta[..., None] * v, (beta * jnp.exp(g))[..., None] * k], -1)
    sol = lax.linalg.triangular_solve(eye + a_mat, rhs, left_side=True, lower=True, unit_diagonal=True)
    u_t, w_t = sol[..., :dv], sol[..., dv:]
    p_mat = jnp.einsum("bhnid,bhnjd->bhnij", q, k) * dec_incl
    qg = q * jnp.exp(g)[..., None]
    g_last = g[..., -1:]
    kd = k * jnp.exp(g_last - g)[..., None]
    gl = jnp.exp(g_last[..., 0])
    xs = tuple(jnp.moveaxis(t, 2, 0) for t in (u_t, w_t, p_mat, qg, kd, gl))

    def step(s, inp):
        ut, wt, pt, qt, kt, gt = inp
        u = ut - jnp.einsum("bhck,bhkv->bhcv", wt, s)
        o = jnp.einsum("bhck,bhkv->bhcv", qt, s) + jnp.einsum("bhij,bhjv->bhiv", pt, u)
        s = gt[..., None, None] * s + jnp.einsum("bhck,bhcv->bhkv", kt, u)
        return s, o

    s_fin, o = lax.scan(step, s0, xs)
    o = jnp.moveaxis(o, 0, 2).reshape(bsz, nh, length, dv)
    return o, s_fin


def deltanet_mixer(h, w_in, conv_w, a_log, dt_bias, norm_g, w_out, s0):
    bsz, length, _ = h.shape
    proj = h @ w_in
    n_qkv = 2 * DN_QK + DN_VD
    qkv = jax.nn.silu(dwconv_seq(proj[..., :n_qkv], conv_w))
    z = proj[..., n_qkv:n_qkv + DN_VD]
    ab = proj[..., n_qkv + DN_VD:].astype(jnp.float32).reshape(bsz, length, 4, DN_HEADS)

    def heads(t, d):
        return t.reshape(bsz, length, DN_HEADS, d).transpose(0, 2, 1, 3).astype(jnp.float32)

    q = l2_normalize(heads(qkv[..., :DN_QK], DN_DK)) * (DN_DK ** -0.5)
    k = l2_normalize(heads(qkv[..., DN_QK:2 * DN_QK], DN_DK))
    v = heads(qkv[..., 2 * DN_QK:], DN_DV)
    log_alpha = -jnp.exp(a_log.astype(jnp.float32)) * jax.nn.softplus(ab[:, :, :2] + dt_bias.astype(jnp.float32))
    beta = jax.nn.sigmoid(ab[:, :, 2:])
    log_alpha = log_alpha.transpose(2, 0, 3, 1)
    beta = beta.transpose(2, 0, 3, 1)
    s0 = s0.astype(jnp.float32)
    o_f, s_f = delta_rule_chunked(q, k, v, log_alpha[0], beta[0], s0[:, 0])
    rev = lambda t: jnp.flip(t, axis=2)
    o_b, s_b = delta_rule_chunked(rev(q), rev(k), rev(v), rev(log_alpha[1]), rev(beta[1]), s0[:, 1])
    o = (o_f + rev(o_b)).transpose(0, 2, 1, 3)
    o = rms_norm(o, norm_g) * jax.nn.silu(z.reshape(bsz, length, DN_HEADS, DN_DV).astype(jnp.float32))
    out = o.reshape(bsz, length, DN_VD).astype(h.dtype) @ w_out
    return out, jnp.stack([s_f, s_b], axis=1)


def run_trunk(x, cond, dn_state, on_grid, w_mod, b_mod, ln_g, ln_b, ffn_w_in, ffn_w_out,
              cv_w1, cv_b1, cv_dw, cv_dwb, cv_ln_g, cv_ln_b, cv_w2, cv_b2,
              dn_w_in, dn_conv, dn_a_log, dn_dt_bias, dn_norm_g, dn_w_out):
    final_states = []
    for l in range(DEPTH):
        m = adaln_params(cond, w_mod[l], b_mod[l])
        y = swiglu_ffn(modulate(x, m, 0), ffn_w_in[l, 0], ffn_w_out[l, 0])
        x = layer_norm(DEEPNORM_ALPHA * x + 0.5 * m[:, :, 2] * y, ln_g[l, 0], ln_b[l, 0])
        h = modulate(x, m, 1)
        j = l // N_MIXERS
        if l % N_MIXERS == 0:
            y = conv_module(h, cv_w1[j], cv_b1[j], cv_dw[j], cv_dwb[j], cv_ln_g[j], cv_ln_b[j],
                            cv_w2[j], cv_b2[j], on_grid)
        else:
            y, s_fin = deltanet_mixer(h, dn_w_in[j], dn_conv[j], dn_a_log[j], dn_dt_bias[j],
                                      dn_norm_g[j], dn_w_out[j], dn_state[:, j])
            final_states.append(s_fin)
        x = layer_norm(DEEPNORM_ALPHA * x + m[:, :, 5] * y, ln_g[l, 1], ln_b[l, 1])
        y = swiglu_ffn(modulate(x, m, 2), ffn_w_in[l, 1], ffn_w_out[l, 1])
        x = layer_norm(DEEPNORM_ALPHA * x + 0.5 * m[:, :, 8] * y, ln_g[l, 2], ln_b[l, 2])
    return x, final_states


def setup_inputs(seed: int = 0) -> dict:
    key = jax.random.key(seed)
    ks = jax.random.split(key, 32)
    f32 = jnp.float32
    nrm = lambda i, shape, s: jax.random.normal(ks[i], shape, f32) * s
    d = D_MODEL
    dt = jnp.exp(jax.random.uniform(ks[24], (N_DN_LAYERS, 2, DN_HEADS), f32, np.log(1e-3), np.log(1e-1)))
    return {
        "x_prompt": nrm(0, (BATCH, SEQ, d), 1.0),
        "x_sample": nrm(1, (DEC_BATCH, DEC_SEQ, d), 1.0),
        "state_delta": nrm(2, (DEC_BATCH, N_DN_LAYERS, 2, DN_HEADS, DN_DK, DN_DV), 0.3),
        "c": nrm(3, (DEC_BATCH, d), 1.0),
        "c_ctx": nrm(4, (d,), 1.0),
        "w_mod": nrm(5, (DEPTH, d, N_MOD * d), 0.5 * d ** -0.5),
        "b_mod": nrm(6, (DEPTH, N_MOD * d), 0.02),
        "ln_g": 1.0 + nrm(7, (DEPTH, 3, d), 0.02),
        "ln_b": nrm(8, (DEPTH, 3, d), 0.02),
        "ffn_w_in": nrm(9, (DEPTH, 2, d, 2 * D_FF), d ** -0.5),
        "ffn_w_out": nrm(10, (DEPTH, 2, D_FF, d), DEEPNORM_BETA * D_FF ** -0.5),
        "cv_w1": nrm(11, (N_CONV_LAYERS, d, 2 * d), d ** -0.5),
        "cv_b1": nrm(12, (N_CONV_LAYERS, 2 * d), 0.02),
        "cv_dw": nrm(13, (N_CONV_LAYERS, CONV_WIDTH, d), CONV_WIDTH ** -0.5),
        "cv_dwb": nrm(14, (N_CONV_LAYERS, d), 0.02),
        "cv_ln_g": 1.0 + nrm(15, (N_CONV_LAYERS, d), 0.02),
        "cv_ln_b": nrm(16, (N_CONV_LAYERS, d), 0.02),
        "cv_w2": nrm(17, (N_CONV_LAYERS, d, d), DEEPNORM_BETA * d ** -0.5),
        "cv_b2": nrm(18, (N_CONV_LAYERS, d), 0.02),
        "dn_w_in": nrm(19, (N_DN_LAYERS, d, DN_PROJ), d ** -0.5),
        "dn_conv": nrm(20, (N_DN_LAYERS, DN_SHORT_CONV, 2 * DN_QK + DN_VD), DN_SHORT_CONV ** -0.5),
        "dn_a_log": jnp.log(jax.random.uniform(ks[21], (N_DN_LAYERS, 2, DN_HEADS), f32, 1.0, 16.0)),
        "dn_dt_bias": jnp.log(jnp.expm1(dt)),
        "dn_norm_g": 1.0 + nrm(22, (N_DN_LAYERS, DN_DV), 0.02),
        "dn_w_out": nrm(23, (N_DN_LAYERS, DN_VD, d), DEEPNORM_BETA * DN_VD ** -0.5),
    }


def reference(x_prompt, x_sample, state_delta, c, c_ctx, w_mod, b_mod, ln_g, ln_b, ffn_w_in, ffn_w_out,
              cv_w1, cv_b1, cv_dw, cv_dwb, cv_ln_g, cv_ln_b, cv_w2, cv_b2,
              dn_w_in, dn_conv, dn_a_log, dn_dt_bias, dn_norm_g, dn_w_out):
    zero_state = jnp.zeros((x_prompt.shape[0], N_DN_LAYERS, 2, DN_HEADS, DN_DK, DN_DV), jnp.float32)
    y_prompt, ctx_states = run_trunk(x_prompt, c_ctx[None, :], zero_state, False, w_mod, b_mod, ln_g, ln_b,
                                     ffn_w_in, ffn_w_out, cv_w1, cv_b1, cv_dw, cv_dwb, cv_ln_g, cv_ln_b,
                                     cv_w2, cv_b2, dn_w_in, dn_conv, dn_a_log, dn_dt_bias, dn_norm_g, dn_w_out)
    y_sample, _ = run_trunk(x_sample, c, state_delta, True, w_mod, b_mod, ln_g, ln_b,
                            ffn_w_in, ffn_w_out, cv_w1, cv_b1, cv_dw, cv_dwb, cv_ln_g, cv_ln_b,
                            cv_w2, cv_b2, dn_w_in, dn_conv, dn_a_log, dn_dt_bias, dn_norm_g, dn_w_out)
    new_state_delta = jnp.stack(ctx_states, axis=1).astype(state_delta.dtype)
    return (y_prompt, y_sample, new_state_delta)
```

```python
import functools

import jax
import jax.numpy as jnp
from jax import lax
from jax.experimental import pallas as pl
from jax.experimental.pallas import tpu as pltpu

F32 = jnp.float32
BF16 = jnp.bfloat16

LN_EPS = 1e-5
L2_EPS = 1e-6
N_MOD = 9
N_MIXERS = 2
GRID_W = 64
DN_HEADS = 8
DN_DK = 128
DN_CHUNK = 64
COND_ROWS = 8
LANES = 128
VMEM_LIMIT = 56 * 1024 * 1024

TOK_TILE = 512
SEQ_BLOCK = 256
FF_CHUNK = 256


def _cparams(n_axes):
    return pltpu.CompilerParams(dimension_semantics=("arbitrary",) * n_axes,
                                vmem_limit_bytes=VMEM_LIMIT)


def _mm(a, b):
    return jnp.dot(a.astype(BF16), b.astype(BF16), preferred_element_type=F32)


def _mm_nt(a, b):
    return lax.dot_general(a.astype(BF16), b.astype(BF16), (((1,), (1,)), ((), ())),
                           preferred_element_type=F32)


def _mm_tn(a, b):
    return lax.dot_general(a.astype(BF16), b.astype(BF16), (((0,), (0,)), ((), ())),
                           preferred_element_type=F32)


def _sigmoid(x):
    return 1.0 / (1.0 + jnp.exp(-x))


def _silu(x):
    return x * _sigmoid(x)


def _layer_norm(r, g, b):
    mu = jnp.mean(r, axis=-1, keepdims=True)
    rc = r - mu
    var = jnp.mean(rc * rc, axis=-1, keepdims=True)
    return rc * lax.rsqrt(var + LN_EPS) * g + b


def _modulate(x, mod_ref, slot):
    shift = mod_ref[3 * slot:3 * slot + 1, :]
    scale = mod_ref[3 * slot + 1:3 * slot + 2, :]
    return x * (1.0 + scale) + shift


def _residual_norm(x, y, gate_scale, mod_ref, slot, g_ref, b_ref, alpha):
    gate = mod_ref[3 * slot + 2:3 * slot + 3, :]
    return _layer_norm(alpha * x + (gate_scale * gate) * y, g_ref[...], b_ref[...])


class _Layout:
    def __init__(self, n_ctx_seq, ctx_len, n_dec_seq, dec_len, d_model):
        self.n_ctx_seq, self.ctx_len = n_ctx_seq, ctx_len
        self.n_dec_seq, self.dec_len = n_dec_seq, dec_len
        self.n_ctx_tok = n_ctx_seq * ctx_len
        self.n_dec_tok = n_dec_seq * dec_len
        self.n_tok = self.n_ctx_tok + self.n_dec_tok
        self.d = d_model
        assert n_dec_seq + 1 <= COND_ROWS
        assert self.n_ctx_tok % TOK_TILE == 0 and dec_len % TOK_TILE == 0
        assert ctx_len == SEQ_BLOCK and dec_len % SEQ_BLOCK == 0
        assert dec_len % GRID_W == 0 and SEQ_BLOCK % GRID_W == 0

    def mod_row(self, i, tile):
        n_ctx_tiles = self.n_ctx_tok // tile
        per_seq = self.dec_len // tile
        return jnp.where(i < n_ctx_tiles, 0, 1 + (i - n_ctx_tiles) // per_seq)

    def mod_spec(self, layer, tile):
        return pl.BlockSpec((None, None, N_MOD, self.d),
                            lambda i: (layer, self.mod_row(i, tile), 0, 0))


def _row_spec(tile, width, col=0):
    return pl.BlockSpec((tile, width), lambda i: (i, col))


def _const_spec(shape):
    return pl.BlockSpec(shape, lambda *_: (0,) * len(shape))


def _adaln_kernel(cond_ref, w_ref, b_ref, o_ref):
    cnd = cond_ref[...]
    o_ref[...] = jnp.dot(_silu(cnd), w_ref[...], preferred_element_type=F32,
                         precision=lax.Precision.HIGHEST) + b_ref[...]


def _adaln(cond, w_mod, b_mod):
    depth, d, n_out = w_mod.shape
    bn = n_out // 4
    return pl.pallas_call(
        _adaln_kernel,
        out_shape=jax.ShapeDtypeStruct((depth, COND_ROWS, n_out), F32),
        grid=(depth, n_out // bn),
        in_specs=[pl.BlockSpec((COND_ROWS, d), lambda l, j: (0, 0)),
                  pl.BlockSpec((None, d, bn), lambda l, j: (l, 0, j)),
                  pl.BlockSpec((None, 1, bn), lambda l, j: (l, 0, j))],
        out_specs=pl.BlockSpec((None, COND_ROWS, bn), lambda l, j: (l, 0, j)),
        compiler_params=_cparams(2),
        name="adaln",
    )(cond, w_mod, b_mod.reshape(depth, 1, n_out))


def _ffn_kernel(x_ref, mod_ref, w_in_ref, w_out_ref, g_ref, b_ref, o_ref, act_ref, *, slot, d_ff, alpha):
    x = x_ref[...]
    h = _modulate(x, mod_ref, slot).astype(BF16)
    for j in range(d_ff // FF_CHUNK):
        lo = j * FF_CHUNK
        lin = jnp.dot(h, w_in_ref[:, lo:lo + FF_CHUNK], preferred_element_type=F32)
        gat = jnp.dot(h, w_in_ref[:, d_ff + lo:d_ff + lo + FF_CHUNK], preferred_element_type=F32)
        act_ref[:, lo:lo + FF_CHUNK] = (_silu(gat) * lin).astype(BF16)
    y = jnp.dot(act_ref[...], w_out_ref[...], preferred_element_type=F32)
    o_ref[...] = _residual_norm(x, y, 0.5, mod_ref, slot, g_ref, b_ref, alpha)


def _ffn(lay, x, mod, layer, slot, w_in, w_out, g, b, alpha):
    d, d_ff = lay.d, w_out.shape[0]
    assert d_ff % FF_CHUNK == 0
    return pl.pallas_call(
        functools.partial(_ffn_kernel, slot=slot, d_ff=d_ff, alpha=alpha),
        out_shape=jax.ShapeDtypeStruct((lay.n_tok, d), F32),
        grid=(lay.n_tok // TOK_TILE,),
        in_specs=[_row_spec(TOK_TILE, d), lay.mod_spec(layer, TOK_TILE),
                  _const_spec((d, 2 * d_ff)), _const_spec((d_ff, d)),
                  _const_spec((1, d)), _const_spec((1, d))],
        out_specs=_row_spec(TOK_TILE, d),
        scratch_shapes=[pltpu.VMEM((TOK_TILE, d_ff), BF16)],
        compiler_params=_cparams(1),
        name="ffn",
    )(x, mod, w_in, w_out, g.reshape(1, d), b.reshape(1, d))


def _conv_in_kernel(x_ref, mod_ref, w1_ref, b1_ref, u_ref, *, d):
    h = _modulate(x_ref[...], mod_ref, 1).astype(BF16)
    lin = jnp.dot(h, w1_ref[:, :d], preferred_element_type=F32) + b1_ref[:, :d]
    gat = jnp.dot(h, w1_ref[:, d:], preferred_element_type=F32) + b1_ref[:, d:]
    u_ref[...] = lin * _sigmoid(gat)


def _conv_in(lay, x, mod, layer, w1, b1):
    d = lay.d
    return pl.pallas_call(
        functools.partial(_conv_in_kernel, d=d),
        out_shape=jax.ShapeDtypeStruct((lay.n_tok, d), F32),
        grid=(lay.n_tok // TOK_TILE,),
        in_specs=[_row_spec(TOK_TILE, d), lay.mod_spec(layer, TOK_TILE),
                  _const_spec((d, 2 * d)), _const_spec((1, 2 * d))],
        out_specs=_row_spec(TOK_TILE, d),
        compiler_params=_cparams(1),
        name="conv_in",
    )(x, mod, w1, b1.reshape(1, 2 * d))


CONV_PAD = 16
CONV_ROWS = 64


def _segconv_kernel(*refs, taps, seg_len, n_seg, aliased):
    if aliased:
        u_ref, w_ref, _, o_ref, pad_ref = refs
    else:
        u_ref, w_ref, o_ref, pad_ref = refs
    n_ch = u_ref.shape[1]
    half = taps // 2
    zeros = jnp.zeros((CONV_PAD, n_ch), F32)
    for s in range(n_seg):
        pad_ref[s, 0:CONV_PAD, :] = zeros
        pad_ref[s, CONV_PAD + seg_len:2 * CONV_PAD + seg_len, :] = zeros
        pad_ref[s, CONV_PAD:CONV_PAD + seg_len, :] = u_ref[s * seg_len:(s + 1) * seg_len, :]

    def lane_tile(j, carry):
        lanes = pl.ds(pl.multiple_of(j * LANES, LANES), LANES)
        for s in range(n_seg):
            for r in range(seg_len // CONV_ROWS):
                acc = jnp.zeros((CONV_ROWS, LANES), F32)
                for k in range(taps):
                    start = CONV_PAD - half + k + r * CONV_ROWS
                    acc = acc + w_ref[k:k + 1, lanes] * pad_ref[s, start:start + CONV_ROWS, lanes]
                o_ref[s * seg_len + r * CONV_ROWS:s * seg_len + (r + 1) * CONV_ROWS, lanes] = acc
        return carry

    lax.fori_loop(0, n_ch // LANES, lane_tile, 0)


def _segconv(u, w, prev, *, n_tok, first_block, n_blocks, seg_len, n_ch, taps):
    n_seg = SEQ_BLOCK // seg_len
    aliased = prev is not None
    in_specs = [pl.BlockSpec((SEQ_BLOCK, n_ch), lambda i: (first_block + i, 0)),
                pl.BlockSpec((w.shape[0], n_ch), lambda i: (0, 0))]
    args = [u, w]
    if aliased:
        in_specs.append(pl.BlockSpec(memory_space=pl.ANY))
        args.append(prev)
    return pl.pallas_call(
        functools.partial(_segconv_kernel, taps=taps, seg_len=seg_len, n_seg=n_seg, aliased=aliased),
        out_shape=jax.ShapeDtypeStruct((n_tok, u.shape[1]), F32),
        grid=(n_blocks,),
        in_specs=in_specs,
        out_specs=pl.BlockSpec((SEQ_BLOCK, n_ch), lambda i: (first_block + i, 0)),
        scratch_shapes=[pltpu.VMEM((n_seg, seg_len + 2 * CONV_PAD, n_ch), F32)],
        input_output_aliases={2: 0} if aliased else {},
        compiler_params=_cparams(1),
        name="segconv",
    )(*args)


VCONV_CH = 256


def _vconv_kernel(u_ref, w_ref, _, o_ref, pad_ref, *, taps, stride):
    seq_len, n_ch = u_ref.shape
    halo = (taps // 2) * stride
    zeros = jnp.zeros((halo, n_ch), F32)
    pad_ref[0:halo, :] = zeros
    pad_ref[halo + seq_len:2 * halo + seq_len, :] = zeros
    pad_ref[halo:halo + seq_len, :] = u_ref[...]

    def row_chunk(r, carry):
        base = pl.multiple_of(r * CONV_ROWS, CONV_ROWS)
        for j in range(n_ch // LANES):
            lanes = slice(j * LANES, (j + 1) * LANES)
            acc = jnp.zeros((CONV_ROWS, LANES), F32)
            for k in range(taps):
                acc = acc + w_ref[k:k + 1, lanes] * pad_ref[pl.ds(base + k * stride, CONV_ROWS), lanes]
            o_ref[pl.ds(base, CONV_ROWS), lanes] = acc
        return carry

    lax.fori_loop(0, seq_len // CONV_ROWS, row_chunk, 0)


def _vconv(lay, u, w, prev, *, first_col_block, n_col_blocks, taps):
    first_seq = lay.n_ctx_tok // lay.dec_len
    halo = (taps // 2) * GRID_W
    return pl.pallas_call(
        functools.partial(_vconv_kernel, taps=taps, stride=GRID_W),
        out_shape=jax.ShapeDtypeStruct((lay.n_tok, lay.d), F32),
        grid=(lay.n_dec_seq, n_col_blocks),
        in_specs=[pl.BlockSpec((lay.dec_len, VCONV_CH), lambda b, j: (first_seq + b, first_col_block + j)),
                  pl.BlockSpec((w.shape[0], VCONV_CH), lambda b, j: (0, first_col_block + j)),
                  pl.BlockSpec(memory_space=pl.ANY)],
        out_specs=pl.BlockSpec((lay.dec_len, VCONV_CH), lambda b, j: (first_seq + b, first_col_block + j)),
        scratch_shapes=[pltpu.VMEM((lay.dec_len + 2 * halo, VCONV_CH), F32)],
        input_output_aliases={2: 0},
        compiler_params=_cparams(2),
        name="vconv",
    )(u, w, prev)


def _conv_out_kernel(x_ref, cv_ref, mod_ref, dwb_ref, lng_ref, lnb_ref, w2_ref, b2_ref, g_ref, b_ref,
                     o_ref, *, alpha):
    a = _silu(_layer_norm(cv_ref[...] + dwb_ref[...], lng_ref[...], lnb_ref[...]))
    y = jnp.dot(a.astype(BF16), w2_ref[...], preferred_element_type=F32) + b2_ref[...]
    o_ref[...] = _residual_norm(x_ref[...], y, 1.0, mod_ref, 1, g_ref, b_ref, alpha)


def _conv_out(lay, x, cv, mod, layer, dwb, lng, lnb, w2, b2, g, b, alpha):
    d = lay.d
    vec = lambda v: v.reshape(1, d)
    return pl.pallas_call(
        functools.partial(_conv_out_kernel, alpha=alpha),
        out_shape=jax.ShapeDtypeStruct((lay.n_tok, d), F32),
        grid=(lay.n_tok // TOK_TILE,),
        in_specs=[_row_spec(TOK_TILE, d), _row_spec(TOK_TILE, d), lay.mod_spec(layer, TOK_TILE),
                  _const_spec((1, d)), _const_spec((1, d)), _const_spec((1, d)),
                  _const_spec((d, d)), _const_spec((1, d)), _const_spec((1, d)), _const_spec((1, d))],
        out_specs=_row_spec(TOK_TILE, d),
        compiler_params=_cparams(1),
        name="conv_out",
    )(x, cv, mod, vec(dwb), vec(lng), vec(lnb), w2, vec(b2), vec(g), vec(b))


def _conv_mixer(lay, x, mod, layer, w1, b1, dw, dwb, lng, lnb, w2, b2, g, b, alpha):
    d = lay.d
    taps = dw.shape[0]
    half_ch = d // 2
    assert taps // 2 < CONV_PAD and half_ch % VCONV_CH == 0
    u = _conv_in(lay, x, mod, layer, w1, b1)
    dw_pad = jnp.pad(dw, ((0, -taps % 8), (0, 0)))
    n_ctx_blocks = lay.n_ctx_tok // SEQ_BLOCK
    n_dec_blocks = lay.n_dec_tok // SEQ_BLOCK
    cv = _segconv(u, dw_pad, None, n_tok=lay.n_tok, first_block=0, n_blocks=n_ctx_blocks,
                  seg_len=lay.ctx_len, n_ch=d, taps=taps)
    cv = _segconv(u, dw_pad, cv, n_tok=lay.n_tok, first_block=n_ctx_blocks, n_blocks=n_dec_blocks,
                  seg_len=GRID_W, n_ch=half_ch, taps=taps)
    cv = _vconv(lay, u, dw_pad, cv, first_col_block=half_ch // VCONV_CH,
                n_col_blocks=(d - half_ch) // VCONV_CH, taps=taps)
    return _conv_out(lay, x, cv, mod, layer, dwb, lng, lnb, w2, b2, g, b, alpha)


def _split3(x):
    hi = x.astype(BF16)
    r1 = x - hi.astype(F32)
    mid = r1.astype(BF16)
    lo = (r1 - mid.astype(F32)).astype(BF16)
    return hi, mid, lo


def _chunk_cumsum(tri, x):
    hi, mid, lo = _split3(x)
    dot = lambda t: jnp.dot(tri, t, preferred_element_type=F32)
    return dot(hi) + dot(mid) + dot(lo)


def _dn_in_kernel(x_ref, mod_ref, w_ref, wab_ref, alog_ref, dtb_ref, proj_ref, gates_ref):
    h = _modulate(x_ref[...], mod_ref, 1).astype(BF16)
    proj_ref[...] = jnp.dot(h, w_ref[...], preferred_element_type=F32)
    ab = jnp.dot(h, wab_ref[...], preferred_element_type=F32)
    pre = ab + dtb_ref[...]
    softplus = jnp.maximum(pre, 0.0) + jnp.log(1.0 + jnp.exp(-jnp.abs(pre)))
    log_a = -jnp.exp(alog_ref[...]) * softplus
    beta = _sigmoid(ab)
    tile = ab.shape[0]
    row = lax.broadcasted_iota(jnp.int32, (tile, tile), 0)
    col = lax.broadcasted_iota(jnp.int32, (tile, tile), 1)
    same = (row // DN_CHUNK) == (col // DN_CHUNK)
    lower = (same & (row >= col)).astype(BF16)
    upper = (same & (row <= col)).astype(BF16)
    lane = lax.broadcasted_iota(jnp.int32, ab.shape, 1)
    g_fwd = _chunk_cumsum(lower, log_a)
    g_bwd = _chunk_cumsum(upper, log_a)
    gates_ref[...] = jnp.where(lane < DN_HEADS, g_fwd, jnp.where(lane < 2 * DN_HEADS, g_bwd, beta))


def _dn_in(lay, x, mod, layer, w_qkvz, w_ab, alog_row, dtb_row):
    d, n_proj = w_qkvz.shape
    return pl.pallas_call(
        _dn_in_kernel,
        out_shape=(jax.ShapeDtypeStruct((lay.n_tok, n_proj), F32),
                   jax.ShapeDtypeStruct((lay.n_tok, LANES), F32)),
        grid=(lay.n_tok // SEQ_BLOCK,),
        in_specs=[_row_spec(SEQ_BLOCK, d), lay.mod_spec(layer, SEQ_BLOCK),
                  _const_spec((d, n_proj)), _const_spec((d, LANES)),
                  _const_spec((1, LANES)), _const_spec((1, LANES))],
        out_specs=(_row_spec(SEQ_BLOCK, n_proj), _row_spec(SEQ_BLOCK, LANES)),
        compiler_params=_cparams(1),
        name="dn_in",
    )(x, mod, w_qkvz, w_ab, alog_row, dtb_row)


def _dn_conv_kernel(p_ref, prev_ref, next_ref, w_ref, o_ref, *, lay, n_qk_tiles):
    i = pl.program_id(0)
    n_ctx_blocks = lay.n_ctx_tok // SEQ_BLOCK
    per_seq = lay.dec_len // SEQ_BLOCK
    pos = (i - n_ctx_blocks) % per_seq
    is_ctx = i < n_ctx_blocks
    has_prev = jnp.logical_not(is_ctx | (pos == 0))
    has_next = jnp.logical_not(is_ctx | (pos == per_seq - 1))
    rows = lax.broadcasted_iota(jnp.int32, (SEQ_BLOCK, LANES), 0)

    def lane_tile(j, carry):
        lanes = pl.ds(pl.multiple_of(j * LANES, LANES), LANES)
        x = p_ref[:, lanes]
        before = jnp.where(has_prev, prev_ref[7:8, lanes], 0.0)
        after = jnp.where(has_next, next_ref[0:1, lanes], 0.0)
        x_m1 = jnp.where(rows == 0, before, pltpu.roll(x, 1, 0))
        x_p1 = jnp.where(rows == SEQ_BLOCK - 1, after, pltpu.roll(x, SEQ_BLOCK - 1, 0))
        y = _silu(w_ref[0:1, lanes] * x_m1 + w_ref[1:2, lanes] * x + w_ref[2:3, lanes] * x_p1)
        inv = lax.rsqrt(jnp.sum(y * y, axis=-1, keepdims=True) + L2_EPS)
        q_scale = jnp.where(j < n_qk_tiles // 2, DN_DK ** -0.5, 1.0)
        o_ref[:, lanes] = jnp.where(j < n_qk_tiles, y * (inv * q_scale), y)
        return carry

    lax.fori_loop(0, o_ref.shape[1] // LANES, lane_tile, 0)


def _dn_conv(lay, proj, conv_w, n_qkv):
    sub = SEQ_BLOCK // 8
    n_units = lay.n_tok // 8
    return pl.pallas_call(
        functools.partial(_dn_conv_kernel, lay=lay, n_qk_tiles=2 * DN_HEADS),
        out_shape=jax.ShapeDtypeStruct((lay.n_tok, n_qkv), F32),
        grid=(lay.n_tok // SEQ_BLOCK,),
        in_specs=[pl.BlockSpec((SEQ_BLOCK, n_qkv), lambda i: (i, 0)),
                  pl.BlockSpec((8, n_qkv), lambda i: (jnp.maximum(i * sub - 1, 0), 0)),
                  pl.BlockSpec((8, n_qkv), lambda i: (jnp.minimum((i + 1) * sub, n_units - 1), 0)),
                  _const_spec((8, n_qkv))],
        out_specs=pl.BlockSpec((SEQ_BLOCK, n_qkv), lambda i: (i, 0)),
        compiler_params=_cparams(1),
        name="dn_conv",
    )(proj, proj, proj, conv_w)


def _tri_masks(lower):
    c = DN_CHUNK
    row = lax.broadcasted_iota(jnp.int32, (c, c), 0)
    col = lax.broadcasted_iota(jnp.int32, (c, c), 1)
    incl = (row >= col) if lower else (row <= col)
    strict = (row > col) if lower else (row < col)
    same = lambda n: (row // n) == (col // n)
    levels = []
    n = 8
    while n < c:
        levels.append(same(2 * n) & jnp.logical_not(same(n)))
        n *= 2
    return incl, strict, same(8), levels


def _unit_tri_inverse_minus_eye(a, base_mask, level_masks):
    d = jnp.where(base_mask, a, 0.0)
    x = _mm(d, d)
    x2 = _mm(x, x)
    m1 = x - d - _mm(d, x)
    n = m1 + x2 + _mm(m1, x2)
    for mask in level_masks:
        e = jnp.where(mask, a, 0.0)
        y = e + _mm(n, e)
        n = n - y - _mm(y, n)
    return n


def _delta_chunk(q, k, v, gcol, bcol, grow, s, lower, masks):
    incl, strict, base_mask, level_masks = masks
    c = DN_CHUNK
    qk_kk = _mm_nt(jnp.concatenate([q, k], axis=0), k)
    decay = jnp.exp(jnp.where(incl, gcol - grow, -jnp.inf))
    a = bcol * qk_kk[c:] * jnp.where(strict, decay, 0.0)
    p = qk_kk[:c] * decay
    n = _unit_tri_inverse_minus_eye(a, base_mask, level_masks)
    eg = jnp.exp(gcol)
    g_last = gcol[c - 1:c] if lower else gcol[0:1]
    rhs = jnp.concatenate([bcol * v, (bcol * eg) * k], axis=1)
    sol = rhs + _mm(n, rhs)
    u_t, w_t = sol[:, :DN_DK], sol[:, DN_DK:]
    qg = q * eg
    kd = k * jnp.exp(g_last - gcol)
    ws = _mm(jnp.concatenate([w_t, qg], axis=0), s)
    u = u_t - ws[:c]
    o = ws[c:] + _mm(p, u)
    s_new = jnp.exp(g_last) * s + _mm_tn(kd, u)
    return o, s_new


def _lane_column(x, lane):
    lanes = lax.broadcasted_iota(jnp.int32, x.shape, 1)
    return jnp.sum(jnp.where(lanes == lane, x, 0.0), axis=-1, keepdims=True)


def _delta_kernel(qf_ref, kf_ref, vf_ref, gf_ref, qb_ref, kb_ref, vb_ref, gb_ref, s0_ref,
                  of_ref, ob_ref, sout_ref, s_ref, grow_ref, *, lay):
    i = pl.program_id(0)
    n_ctx_blocks = lay.n_ctx_tok // SEQ_BLOCK
    per_seq = lay.dec_len // SEQ_BLOCK
    n_chunks = SEQ_BLOCK // DN_CHUNK
    is_ctx = i < n_ctx_blocks
    seq_start = is_ctx | ((i - n_ctx_blocks) % per_seq == 0)

    @pl.when(seq_start)
    def _():
        s_ref[...] = jnp.where(is_ctx, 0.0, s0_ref[...])

    views = ((qf_ref, kf_ref, vf_ref, gf_ref, of_ref), (qb_ref, kb_ref, vb_ref, gb_ref, ob_ref))
    for d, view in enumerate(views):
        gates_t = jnp.transpose(view[3][...])
        for c in range(n_chunks):
            grow_ref[d, c] = gates_t[:, c * DN_CHUNK:(c + 1) * DN_CHUNK]
    masks = (_tri_masks(True), _tri_masks(False))

    def chunk_step(c, carry):
        for d, (q_ref, k_ref, v_ref, g_ref, o_ref) in enumerate(views):
            cc = c if d == 0 else n_chunks - 1 - c
            rows = pl.ds(pl.multiple_of(cc * DN_CHUNK, DN_CHUNK), DN_CHUNK)
            gates = g_ref[rows, :]
            for h in range(DN_HEADS):
                lanes = slice(h * DN_DK, (h + 1) * DN_DK)
                lane = d * DN_HEADS + h
                gcol = _lane_column(gates, lane)
                bcol = _lane_column(gates, 2 * DN_HEADS + lane)
                grow = grow_ref[d, cc, lane:lane + 1, :]
                o, s_new = _delta_chunk(q_ref[rows, lanes], k_ref[rows, lanes], v_ref[rows, lanes],
                                        gcol, bcol, grow, s_ref[d, h], d == 0, masks[d])
                o_ref[rows, lanes] = o
                s_ref[d, h] = s_new
        return carry

    lax.fori_loop(0, n_chunks, chunk_step, 0)
    sout_ref[...] = s_ref[...]


def _delta_rule(lay, qkv, gates, s0, layer_j):
    n_blocks = lay.n_tok // SEQ_BLOCK
    n_ctx_blocks = lay.n_ctx_tok // SEQ_BLOCK
    per_seq = lay.dec_len // SEQ_BLOCK
    vd = DN_HEADS * DN_DK

    def rev(i):
        j = i - n_ctx_blocks
        return jnp.where(i < n_ctx_blocks, i, n_ctx_blocks + (j // per_seq) * per_seq + per_seq - 1 - j % per_seq)

    fwd = lambda i: i
    cols = lambda blk, part: pl.BlockSpec((SEQ_BLOCK, vd), lambda i: (blk(i), part))
    gate_spec = lambda blk: pl.BlockSpec((SEQ_BLOCK, LANES), lambda i: (blk(i), 0))
    state_shape = (2, DN_HEADS, DN_DK, DN_DK)
    s0_spec = pl.BlockSpec((None, None) + state_shape,
                           lambda i: (jnp.maximum(i - n_ctx_blocks, 0) // per_seq, layer_j, 0, 0, 0, 0))
    sout_spec = pl.BlockSpec((None,) + state_shape,
                             lambda i: (jnp.minimum(i, n_ctx_blocks), 0, 0, 0, 0))
    o_shape = jax.ShapeDtypeStruct((lay.n_tok, vd), F32)
    return pl.pallas_call(
        functools.partial(_delta_kernel, lay=lay),
        out_shape=(o_shape, o_shape,
                   jax.ShapeDtypeStruct((n_ctx_blocks + 1,) + state_shape, F32)),
        grid=(n_blocks,),
        in_specs=[cols(fwd, 0), cols(fwd, 1), cols(fwd, 2), gate_spec(fwd),
                  cols(rev, 0), cols(rev, 1), cols(rev, 2), gate_spec(rev), s0_spec],
        out_specs=(cols(fwd, 0), cols(rev, 0), sout_spec),
        scratch_shapes=[pltpu.VMEM(state_shape, F32),
                        pltpu.VMEM((2, SEQ_BLOCK // DN_CHUNK, LANES, DN_CHUNK), F32)],
        compiler_params=_cparams(1),
        name="delta_rule",
    )(qkv, qkv, qkv, gates, qkv, qkv, qkv, gates, s0)


def _dn_out_kernel(x_ref, of_ref, ob_ref, z_ref, mod_ref, ng_ref, w_ref, g_ref, b_ref, o_ref, act_ref,
                   *, alpha):
    for h in range(DN_HEADS):
        lanes = slice(h * DN_DK, (h + 1) * DN_DK)
        o = of_ref[:, lanes] + ob_ref[:, lanes]
        rms = lax.rsqrt(jnp.mean(o * o, axis=-1, keepdims=True) + LN_EPS)
        act_ref[:, lanes] = (o * rms * ng_ref[...] * _silu(z_ref[:, lanes])).astype(BF16)
    y = jnp.dot(act_ref[...], w_ref[...], preferred_element_type=F32)
    o_ref[...] = _residual_norm(x_ref[...], y, 1.0, mod_ref, 1, g_ref, b_ref, alpha)


def _dn_out(lay, x, o_f, o_b, proj, mod, layer, norm_g, w_out, g, b, alpha):
    d = lay.d
    vd = DN_HEADS * DN_DK
    z_col = (proj.shape[1] - vd) // vd
    return pl.pallas_call(
        functools.partial(_dn_out_kernel, alpha=alpha),
        out_shape=jax.ShapeDtypeStruct((lay.n_tok, d), F32),
        grid=(lay.n_tok // TOK_TILE,),
        in_specs=[_row_spec(TOK_TILE, d), _row_spec(TOK_TILE, vd), _row_spec(TOK_TILE, vd),
                  _row_spec(TOK_TILE, vd, z_col), lay.mod_spec(layer, TOK_TILE),
                  _const_spec((1, DN_DK)), _const_spec((vd, d)), _const_spec((1, d)), _const_spec((1, d))],
        out_specs=_row_spec(TOK_TILE, d),
        scratch_shapes=[pltpu.VMEM((TOK_TILE, vd), BF16)],
        compiler_params=_cparams(1),
        name="dn_out",
    )(x, o_f, o_b, proj, mod, norm_g.reshape(1, DN_DK), w_out, g.reshape(1, d), b.reshape(1, d))


def _dn_mixer(lay, x, mod, layer, w_in, conv_w, a_log, dt_bias, norm_g, w_out, s0, layer_j, g, b, alpha):
    qk, vd = DN_HEADS * DN_DK, DN_HEADS * DN_DK
    n_qkv = 2 * qk + vd
    n_gate = 4 * DN_HEADS
    assert w_in.shape[1] == n_qkv + vd + n_gate and n_gate <= LANES
    w_qkvz = w_in[:, :n_qkv + vd].astype(BF16)
    w_ab = jnp.pad(w_in[:, n_qkv + vd:], ((0, 0), (0, LANES - n_gate))).astype(BF16)
    lane_row = lambda v: jnp.pad(v.reshape(1, 2 * DN_HEADS), ((0, 0), (0, LANES - 2 * DN_HEADS)))
    proj, gates = _dn_in(lay, x, mod, layer, w_qkvz, w_ab, lane_row(a_log), lane_row(dt_bias))
    conv_pad = jnp.pad(conv_w, ((0, 8 - conv_w.shape[0]), (0, 0)))
    qkv = _dn_conv(lay, proj, conv_pad, n_qkv)
    o_f, o_b, s_fin = _delta_rule(lay, qkv, gates, s0, layer_j)
    x = _dn_out(lay, x, o_f, o_b, proj, mod, layer, norm_g, w_out, g, b, alpha)
    return x, s_fin[:lay.n_ctx_seq]


def kernel(x_prompt, x_sample, state_delta, c, c_ctx, w_mod, b_mod, ln_g, ln_b, ffn_w_in, ffn_w_out,
           cv_w1, cv_b1, cv_dw, cv_dwb, cv_ln_g, cv_ln_b, cv_w2, cv_b2,
           dn_w_in, dn_conv, dn_a_log, dn_dt_bias, dn_norm_g, dn_w_out):
    n_ctx_seq, ctx_len, d = x_prompt.shape
    n_dec_seq, dec_len, _ = x_sample.shape
    depth = w_mod.shape[0]
    alpha = (2.0 * depth) ** 0.25
    lay = _Layout(n_ctx_seq, ctx_len, n_dec_seq, dec_len, d)

    x = jnp.concatenate([x_prompt.reshape(lay.n_ctx_tok, d), x_sample.reshape(lay.n_dec_tok, d)], axis=0)
    cond = jnp.concatenate([c_ctx[None, :], c, jnp.zeros((COND_ROWS - 1 - n_dec_seq, d), F32)], axis=0)
    mod = _adaln(cond, w_mod, b_mod).reshape(depth, COND_ROWS, N_MOD, d)

    states = []
    for l in range(depth):
        x = _ffn(lay, x, mod, l, 0, ffn_w_in[l, 0].astype(BF16), ffn_w_out[l, 0].astype(BF16),
                 ln_g[l, 0], ln_b[l, 0], alpha)
        j = l // N_MIXERS
        if l % N_MIXERS == 0:
            x = _conv_mixer(lay, x, mod, l, cv_w1[j].astype(BF16), cv_b1[j], cv_dw[j], cv_dwb[j],
                            cv_ln_g[j], cv_ln_b[j], cv_w2[j].astype(BF16), cv_b2[j],
                            ln_g[l, 1], ln_b[l, 1], alpha)
        else:
            x, s_fin = _dn_mixer(lay, x, mod, l, dn_w_in[j], dn_conv[j], dn_a_log[j], dn_dt_bias[j],
                                 dn_norm_g[j], dn_w_out[j].astype(BF16), state_delta, j,
                                 ln_g[l, 1], ln_b[l, 1], alpha)
            states.append(s_fin)
        x = _ffn(lay, x, mod, l, 2, ffn_w_in[l, 1].astype(BF16), ffn_w_out[l, 1].astype(BF16),
                 ln_g[l, 2], ln_b[l, 2], alpha)

    y_prompt = x[:lay.n_ctx_tok].reshape(n_ctx_seq, ctx_len, d)
    y_sample = x[lay.n_ctx_tok:].reshape(n_dec_seq, dec_len, d)
    new_state = jnp.stack(states, axis=1).astype(state_delta.dtype)
    return y_prompt, y_sample, new_state
```

```python
import functools

import jax
import jax.numpy as jnp
from jax import lax
from jax.experimental import pallas as pl
from jax.experimental.pallas import tpu as pltpu

F32 = jnp.float32
BF16 = jnp.bfloat16

LN_EPS = 1e-5
L2_EPS = 1e-6
N_MOD = 9
N_MIXERS = 2
GRID_W = 64
DN_HEADS = 8
DN_DK = 128
DN_CHUNK = 64
COND_ROWS = 8
LANES = 128
VMEM_LIMIT = 56 * 1024 * 1024

TOK_TILE = 512
SEQ_BLOCK = 256
FF_CHUNK = 256


def _cparams(n_axes):
    return pltpu.CompilerParams(dimension_semantics=("arbitrary",) * n_axes,
                                vmem_limit_bytes=VMEM_LIMIT)


def _mm(a, b):
    return jnp.dot(a.astype(BF16), b.astype(BF16), preferred_element_type=F32)


def _mm_nt(a, b):
    return lax.dot_general(a.astype(BF16), b.astype(BF16), (((1,), (1,)), ((), ())),
                           preferred_element_type=F32)


def _mm_tn(a, b):
    return lax.dot_general(a.astype(BF16), b.astype(BF16), (((0,), (0,)), ((), ())),
                           preferred_element_type=F32)


def _sigmoid(x):
    return 1.0 / (1.0 + jnp.exp(-x))


def _silu(x):
    return x * _sigmoid(x)


def _layer_norm(r, g, b):
    mu = jnp.mean(r, axis=-1, keepdims=True)
    rc = r - mu
    var = jnp.mean(rc * rc, axis=-1, keepdims=True)
    return rc * lax.rsqrt(var + LN_EPS) * g + b


def _modulate(x, mod_ref, slot):
    shift = mod_ref[3 * slot:3 * slot + 1, :]
    scale = mod_ref[3 * slot + 1:3 * slot + 2, :]
    return x * (1.0 + scale) + shift


def _residual_norm(x, y, gate_scale, mod_ref, slot, g_ref, b_ref, alpha):
    gate = mod_ref[3 * slot + 2:3 * slot + 3, :]
    return _layer_norm(alpha * x + (gate_scale * gate) * y, g_ref[...], b_ref[...])


class _Layout:
    def __init__(self, n_ctx_seq, ctx_len, n_dec_seq, dec_len, d_model):
        self.n_ctx_seq, self.ctx_len = n_ctx_seq, ctx_len
        self.n_dec_seq, self.dec_len = n_dec_seq, dec_len
        self.n_ctx_tok = n_ctx_seq * ctx_len
        self.n_dec_tok = n_dec_seq * dec_len
        self.n_tok = self.n_ctx_tok + self.n_dec_tok
        self.d = d_model
        assert n_dec_seq + 1 <= COND_ROWS
        assert self.n_ctx_tok % TOK_TILE == 0 and dec_len % TOK_TILE == 0
        assert ctx_len == SEQ_BLOCK and dec_len % SEQ_BLOCK == 0
        assert dec_len % GRID_W == 0 and SEQ_BLOCK % GRID_W == 0

    def mod_row(self, i, tile):
        n_ctx_tiles = self.n_ctx_tok // tile
        per_seq = self.dec_len // tile
        return jnp.where(i < n_ctx_tiles, 0, 1 + (i - n_ctx_tiles) // per_seq)

    def mod_spec(self, layer, tile):
        return pl.BlockSpec((None, None, N_MOD, self.d),
                            lambda i: (layer, self.mod_row(i, tile), 0, 0))


def _row_spec(tile, width, col=0):
    return pl.BlockSpec((tile, width), lambda i: (i, col))


def _const_spec(shape):
    return pl.BlockSpec(shape, lambda *_: (0,) * len(shape))


def _adaln_kernel(cond_ref, w_ref, b_ref, o_ref):
    cnd = cond_ref[...]
    o_ref[...] = jnp.dot(_silu(cnd), w_ref[...], preferred_element_type=F32,
                         precision=lax.Precision.HIGHEST) + b_ref[...]


def _adaln(cond, w_mod, b_mod):
    depth, d, n_out = w_mod.shape
    bn = n_out // 4
    return pl.pallas_call(
        _adaln_kernel,
        out_shape=jax.ShapeDtypeStruct((depth, COND_ROWS, n_out), F32),
        grid=(depth, n_out // bn),
        in_specs=[pl.BlockSpec((COND_ROWS, d), lambda l, j: (0, 0)),
                  pl.BlockSpec((None, d, bn), lambda l, j: (l, 0, j)),
                  pl.BlockSpec((None, 1, bn), lambda l, j: (l, 0, j))],
        out_specs=pl.BlockSpec((None, COND_ROWS, bn), lambda l, j: (l, 0, j)),
        compiler_params=_cparams(2),
        name="adaln",
    )(cond, w_mod, b_mod.reshape(depth, 1, n_out))


def _ffn_kernel(x_ref, mod_ref, w_in_ref, w_out_ref, g_ref, b_ref, o_ref, act_ref, *, slot, d_ff, alpha):
    x = x_ref[...]
    h = _modulate(x, mod_ref, slot).astype(BF16)
    for j in range(d_ff // FF_CHUNK):
        lo = j * FF_CHUNK
        lin = jnp.dot(h, w_in_ref[:, lo:lo + FF_CHUNK], preferred_element_type=F32)
        gat = jnp.dot(h, w_in_ref[:, d_ff + lo:d_ff + lo + FF_CHUNK], preferred_element_type=F32)
        act_ref[:, lo:lo + FF_CHUNK] = (_silu(gat) * lin).astype(BF16)
    y = jnp.dot(act_ref[...], w_out_ref[...], preferred_element_type=F32)
    o_ref[...] = _residual_norm(x, y, 0.5, mod_ref, slot, g_ref, b_ref, alpha)


def _ffn(lay, x, mod, layer, slot, w_in, w_out, g, b, alpha):
    d, d_ff = lay.d, w_out.shape[0]
    assert d_ff % FF_CHUNK == 0
    return pl.pallas_call(
        functools.partial(_ffn_kernel, slot=slot, d_ff=d_ff, alpha=alpha),
        out_shape=jax.ShapeDtypeStruct((lay.n_tok, d), F32),
        grid=(lay.n_tok // TOK_TILE,),
        in_specs=[_row_spec(TOK_TILE, d), lay.mod_spec(layer, TOK_TILE),
                  _const_spec((d, 2 * d_ff)), _const_spec((d_ff, d)),
                  _const_spec((1, d)), _const_spec((1, d))],
        out_specs=_row_spec(TOK_TILE, d),
        scratch_shapes=[pltpu.VMEM((TOK_TILE, d_ff), BF16)],
        compiler_params=_cparams(1),
        name="ffn",
    )(x, mod, w_in, w_out, g.reshape(1, d), b.reshape(1, d))


def _conv_in_kernel(x_ref, mod_ref, w1_ref, b1_ref, u_ref, *, d):
    h = _modulate(x_ref[...], mod_ref, 1).astype(BF16)
    lin = jnp.dot(h, w1_ref[:, :d], preferred_element_type=F32) + b1_ref[:, :d]
    gat = jnp.dot(h, w1_ref[:, d:], preferred_element_type=F32) + b1_ref[:, d:]
    u_ref[...] = lin * _sigmoid(gat)


def _conv_in(lay, x, mod, layer, w1, b1):
    d = lay.d
    return pl.pallas_call(
        functools.partial(_conv_in_kernel, d=d),
        out_shape=jax.ShapeDtypeStruct((lay.n_tok, d), F32),
        grid=(lay.n_tok // TOK_TILE,),
        in_specs=[_row_spec(TOK_TILE, d), lay.mod_spec(layer, TOK_TILE),
                  _const_spec((d, 2 * d)), _const_spec((1, 2 * d))],
        out_specs=_row_spec(TOK_TILE, d),
        compiler_params=_cparams(1),
        name="conv_in",
    )(x, mod, w1, b1.reshape(1, 2 * d))


CONV_PAD = 16
CONV_ROWS = 64


def _segconv_kernel(*refs, taps, seg_len, n_seg, aliased):
    if aliased:
        u_ref, w_ref, _, o_ref, pad_ref = refs
    else:
        u_ref, w_ref, o_ref, pad_ref = refs
    n_ch = u_ref.shape[1]
    half = taps // 2
    zeros = jnp.zeros((CONV_PAD, n_ch), F32)
    for s in range(n_seg):
        pad_ref[s, 0:CONV_PAD, :] = zeros
        pad_ref[s, CONV_PAD + seg_len:2 * CONV_PAD + seg_len, :] = zeros
        pad_ref[s, CONV_PAD:CONV_PAD + seg_len, :] = u_ref[s * seg_len:(s + 1) * seg_len, :]

    def lane_tile(j, carry):
        lanes = pl.ds(pl.multiple_of(j * LANES, LANES), LANES)
        for s in range(n_seg):
            for r in range(seg_len // CONV_ROWS):
                acc = jnp.zeros((CONV_ROWS, LANES), F32)
                for k in range(taps):
                    start = CONV_PAD - half + k + r * CONV_ROWS
                    acc = acc + w_ref[k:k + 1, lanes] * pad_ref[s, start:start + CONV_ROWS, lanes]
                o_ref[s * seg_len + r * CONV_ROWS:s * seg_len + (r + 1) * CONV_ROWS, lanes] = acc
        return carry

    lax.fori_loop(0, n_ch // LANES, lane_tile, 0)


def _segconv(u, w, prev, *, n_tok, first_block, n_blocks, seg_len, n_ch, taps):
    n_seg = SEQ_BLOCK // seg_len
    aliased = prev is not None
    in_specs = [pl.BlockSpec((SEQ_BLOCK, n_ch), lambda i: (first_block + i, 0)),
                pl.BlockSpec((w.shape[0], n_ch), lambda i: (0, 0))]
    args = [u, w]
    if aliased:
        in_specs.append(pl.BlockSpec(memory_space=pl.ANY))
        args.append(prev)
    return pl.pallas_call(
        functools.partial(_segconv_kernel, taps=taps, seg_len=seg_len, n_seg=n_seg, aliased=aliased),
        out_shape=jax.ShapeDtypeStruct((n_tok, u.shape[1]), F32),
        grid=(n_blocks,),
        in_specs=in_specs,
        out_specs=pl.BlockSpec((SEQ_BLOCK, n_ch), lambda i: (first_block + i, 0)),
        scratch_shapes=[pltpu.VMEM((n_seg, seg_len + 2 * CONV_PAD, n_ch), F32)],
        input_output_aliases={2: 0} if aliased else {},
        compiler_params=_cparams(1),
        name="segconv",
    )(*args)


VCONV_CH = 256


def _vconv_kernel(u_ref, w_ref, _, o_ref, pad_ref, *, taps, stride):
    seq_len, n_ch = u_ref.shape
    halo = (taps // 2) * stride
    zeros = jnp.zeros((halo, n_ch), F32)
    pad_ref[0:halo, :] = zeros
    pad_ref[halo + seq_len:2 * halo + seq_len, :] = zeros
    pad_ref[halo:halo + seq_len, :] = u_ref[...]

    def row_chunk(r, carry):
        base = pl.multiple_of(r * CONV_ROWS, CONV_ROWS)
        for j in range(n_ch // LANES):
            lanes = slice(j * LANES, (j + 1) * LANES)
            acc = jnp.zeros((CONV_ROWS, LANES), F32)
            for k in range(taps):
                acc = acc + w_ref[k:k + 1, lanes] * pad_ref[pl.ds(base + k * stride, CONV_ROWS), lanes]
            o_ref[pl.ds(base, CONV_ROWS), lanes] = acc
        return carry

    lax.fori_loop(0, seq_len // CONV_ROWS, row_chunk, 0)


def _vconv(lay, u, w, prev, *, first_col_block, n_col_blocks, taps):
    first_seq = lay.n_ctx_tok // lay.dec_len
    halo = (taps // 2) * GRID_W
    return pl.pallas_call(
        functools.partial(_vconv_kernel, taps=taps, stride=GRID_W),
        out_shape=jax.ShapeDtypeStruct((lay.n_tok, lay.d), F32),
        grid=(lay.n_dec_seq, n_col_blocks),
        in_specs=[pl.BlockSpec((lay.dec_len, VCONV_CH), lambda b, j: (first_seq + b, first_col_block + j)),
                  pl.BlockSpec((w.shape[0], VCONV_CH), lambda b, j: (0, first_col_block + j)),
                  pl.BlockSpec(memory_space=pl.ANY)],
        out_specs=pl.BlockSpec((lay.dec_len, VCONV_CH), lambda b, j: (first_seq + b, first_col_block + j)),
        scratch_shapes=[pltpu.VMEM((lay.dec_len + 2 * halo, VCONV_CH), F32)],
        input_output_aliases={2: 0},
        compiler_params=_cparams(2),
        name="vconv",
    )(u, w, prev)


def _conv_out_kernel(x_ref, cv_ref, mod_ref, dwb_ref, lng_ref, lnb_ref, w2_ref, b2_ref, g_ref, b_ref,
                     o_ref, *, alpha):
    a = _silu(_layer_norm(cv_ref[...] + dwb_ref[...], lng_ref[...], lnb_ref[...]))
    y = jnp.dot(a.astype(BF16), w2_ref[...], preferred_element_type=F32) + b2_ref[...]
    o_ref[...] = _residual_norm(x_ref[...], y, 1.0, mod_ref, 1, g_ref, b_ref, alpha)


def _conv_out(lay, x, cv, mod, layer, dwb, lng, lnb, w2, b2, g, b, alpha):
    d = lay.d
    vec = lambda v: v.reshape(1, d)
    return pl.pallas_call(
        functools.partial(_conv_out_kernel, alpha=alpha),
        out_shape=jax.ShapeDtypeStruct((lay.n_tok, d), F32),
        grid=(lay.n_tok // TOK_TILE,),
        in_specs=[_row_spec(TOK_TILE, d), _row_spec(TOK_TILE, d), lay.mod_spec(layer, TOK_TILE),
                  _const_spec((1, d)), _const_spec((1, d)), _const_spec((1, d)),
                  _const_spec((d, d)), _const_spec((1, d)), _const_spec((1, d)), _const_spec((1, d))],
        out_specs=_row_spec(TOK_TILE, d),
        compiler_params=_cparams(1),
        name="conv_out",
    )(x, cv, mod, vec(dwb), vec(lng), vec(lnb), w2, vec(b2), vec(g), vec(b))


def _conv_mixer(lay, x, mod, layer, w1, b1, dw, dwb, lng, lnb, w2, b2, g, b, alpha):
    d = lay.d
    taps = dw.shape[0]
    half_ch = d // 2
    assert taps // 2 < CONV_PAD and half_ch % VCONV_CH == 0
    u = _conv_in(lay, x, mod, layer, w1, b1)
    dw_pad = jnp.pad(dw, ((0, -taps % 8), (0, 0)))
    n_ctx_blocks = lay.n_ctx_tok // SEQ_BLOCK
    n_dec_blocks = lay.n_dec_tok // SEQ_BLOCK
    cv = _segconv(u, dw_pad, None, n_tok=lay.n_tok, first_block=0, n_blocks=n_ctx_blocks,
                  seg_len=lay.ctx_len, n_ch=d, taps=taps)
    cv = _segconv(u, dw_pad, cv, n_tok=lay.n_tok, first_block=n_ctx_blocks, n_blocks=n_dec_blocks,
                  seg_len=GRID_W, n_ch=half_ch, taps=taps)
    cv = _vconv(lay, u, dw_pad, cv, first_col_block=half_ch // VCONV_CH,
                n_col_blocks=(d - half_ch) // VCONV_CH, taps=taps)
    return _conv_out(lay, x, cv, mod, layer, dwb, lng, lnb, w2, b2, g, b, alpha)


def _split3(x):
    hi = x.astype(BF16)
    r1 = x - hi.astype(F32)
    mid = r1.astype(BF16)
    lo = (r1 - mid.astype(F32)).astype(BF16)
    return hi, mid, lo


def _chunk_cumsum(tri, x):
    hi, mid, lo = _split3(x)
    dot = lambda t: jnp.dot(tri, t, preferred_element_type=F32)
    return dot(hi) + dot(mid) + dot(lo)


def _dn_in_kernel(x_ref, mod_ref, w_ref, wab_ref, alog_ref, dtb_ref, proj_ref, gates_ref):
    h = _modulate(x_ref[...], mod_ref, 1).astype(BF16)
    proj_ref[...] = jnp.dot(h, w_ref[...], preferred_element_type=F32)
    ab = jnp.dot(h, wab_ref[...], preferred_element_type=F32)
    pre = ab + dtb_ref[...]
    softplus = jnp.maximum(pre, 0.0) + jnp.log(1.0 + jnp.exp(-jnp.abs(pre)))
    log_a = -jnp.exp(alog_ref[...]) * softplus
    beta = _sigmoid(ab)
    tile = ab.shape[0]
    row = lax.broadcasted_iota(jnp.int32, (tile, tile), 0)
    col = lax.broadcasted_iota(jnp.int32, (tile, tile), 1)
    same = (row // DN_CHUNK) == (col // DN_CHUNK)
    lower = (same & (row >= col)).astype(BF16)
    upper = (same & (row <= col)).astype(BF16)
    lane = lax.broadcasted_iota(jnp.int32, ab.shape, 1)
    g_fwd = _chunk_cumsum(lower, log_a)
    g_bwd = _chunk_cumsum(upper, log_a)
    gates_ref[...] = jnp.where(lane < DN_HEADS, g_fwd, jnp.where(lane < 2 * DN_HEADS, g_bwd, beta))


def _dn_in(lay, x, mod, layer, w_qkvz, w_ab, alog_row, dtb_row):
    d, n_proj = w_qkvz.shape
    return pl.pallas_call(
        _dn_in_kernel,
        out_shape=(jax.ShapeDtypeStruct((lay.n_tok, n_proj), F32),
                   jax.ShapeDtypeStruct((lay.n_tok, LANES), F32)),
        grid=(lay.n_tok // SEQ_BLOCK,),
        in_specs=[_row_spec(SEQ_BLOCK, d), lay.mod_spec(layer, SEQ_BLOCK),
                  _const_spec((d, n_proj)), _const_spec((d, LANES)),
                  _const_spec((1, LANES)), _const_spec((1, LANES))],
        out_specs=(_row_spec(SEQ_BLOCK, n_proj), _row_spec(SEQ_BLOCK, LANES)),
        compiler_params=_cparams(1),
        name="dn_in",
    )(x, mod, w_qkvz, w_ab, alog_row, dtb_row)


def _dn_conv_kernel(p_ref, prev_ref, next_ref, w_ref, o_ref, *, lay, n_qk_tiles):
    i = pl.program_id(0)
    n_ctx_blocks = lay.n_ctx_tok // SEQ_BLOCK
    per_seq = lay.dec_len // SEQ_BLOCK
    pos = (i - n_ctx_blocks) % per_seq
    is_ctx = i < n_ctx_blocks
    has_prev = jnp.logical_not(is_ctx | (pos == 0))
    has_next = jnp.logical_not(is_ctx | (pos == per_seq - 1))
    rows = lax.broadcasted_iota(jnp.int32, (SEQ_BLOCK, LANES), 0)

    def lane_tile(j, carry):
        lanes = pl.ds(pl.multiple_of(j * LANES, LANES), LANES)
        x = p_ref[:, lanes]
        before = jnp.where(has_prev, prev_ref[7:8, lanes], 0.0)
        after = jnp.where(has_next, next_ref[0:1, lanes], 0.0)
        x_m1 = jnp.where(rows == 0, before, pltpu.roll(x, 1, 0))
        x_p1 = jnp.where(rows == SEQ_BLOCK - 1, after, pltpu.roll(x, SEQ_BLOCK - 1, 0))
        y = _silu(w_ref[0:1, lanes] * x_m1 + w_ref[1:2, lanes] * x + w_ref[2:3, lanes] * x_p1)
        inv = lax.rsqrt(jnp.sum(y * y, axis=-1, keepdims=True) + L2_EPS)
        q_scale = jnp.where(j < n_qk_tiles // 2, DN_DK ** -0.5, 1.0)
        o_ref[:, lanes] = jnp.where(j < n_qk_tiles, y * (inv * q_scale), y)
        return carry

    lax.fori_loop(0, o_ref.shape[1] // LANES, lane_tile, 0)


def _dn_conv(lay, proj, conv_w, n_qkv):
    sub = SEQ_BLOCK // 8
    n_units = lay.n_tok // 8
    return pl.pallas_call(
        functools.partial(_dn_conv_kernel, lay=lay, n_qk_tiles=2 * DN_HEADS),
        out_shape=jax.ShapeDtypeStruct((lay.n_tok, n_qkv), F32),
        grid=(lay.n_tok // SEQ_BLOCK,),
        in_specs=[pl.BlockSpec((SEQ_BLOCK, n_qkv), lambda i: (i, 0)),
                  pl.BlockSpec((8, n_qkv), lambda i: (jnp.maximum(i * sub - 1, 0), 0)),
                  pl.BlockSpec((8, n_qkv), lambda i: (jnp.minimum((i + 1) * sub, n_units - 1), 0)),
                  _const_spec((8, n_qkv))],
        out_specs=pl.BlockSpec((SEQ_BLOCK, n_qkv), lambda i: (i, 0)),
        compiler_params=_cparams(1),
        name="dn_conv",
    )(proj, proj, proj, conv_w)


def _bmm(a, b):
    return jnp.einsum("bmk,bkn->bmn", a.astype(BF16), b.astype(BF16), preferred_element_type=F32)


def _bmm_nt(a, b):
    return jnp.einsum("bmk,bnk->bmn", a.astype(BF16), b.astype(BF16), preferred_element_type=F32)


def _bmm_tn(a, b):
    return jnp.einsum("bkm,bkn->bmn", a.astype(BF16), b.astype(BF16), preferred_element_type=F32)


def _chunk_masks():
    c = DN_CHUNK
    row = lax.broadcasted_iota(jnp.int32, (c, c), 0)
    col = lax.broadcasted_iota(jnp.int32, (c, c), 1)
    same = lambda n: (row // n) == (col // n)
    levels = []
    n = 8
    while n < c:
        levels.append(same(2 * n) & jnp.logical_not(same(n)))
        n *= 2
    return row, col, same(8), levels


def _dir_where(mask_fwd, mask_bwd, x, other):
    half = x.shape[0] // 2
    return jnp.concatenate([jnp.where(mask_fwd[None], x[:half], other),
                            jnp.where(mask_bwd[None], x[half:], other)], axis=0)


def _unit_tri_inverse_minus_eye(a, base_mask, level_masks):
    d = jnp.where(base_mask[None], a, 0.0)
    x = _bmm(d, d)
    x2 = _bmm(x, x)
    m1 = x - d - _bmm(d, x)
    n = m1 + x2 + _bmm(m1, x2)
    for mask in level_masks:
        e = jnp.where(mask[None], a, 0.0)
        y = e + _bmm(n, e)
        n = n - y - _bmm(y, n)
    return n


def _lane_broadcast(x, lane):
    lanes = lax.broadcasted_iota(jnp.int32, x.shape, 1)
    col = jnp.sum(jnp.where(lanes == lane, x, 0.0), axis=-1, keepdims=True)
    return jnp.broadcast_to(col, x.shape)


def _delta_kernel(qf_ref, kf_ref, vf_ref, gf_ref, qb_ref, kb_ref, vb_ref, gb_ref, s0_ref,
                  of_ref, ob_ref, sout_ref, s_ref, *, lay):
    i = pl.program_id(0)
    n_ctx_blocks = lay.n_ctx_tok // SEQ_BLOCK
    per_seq = lay.dec_len // SEQ_BLOCK
    nc, nh, c = SEQ_BLOCK // DN_CHUNK, DN_HEADS, DN_CHUNK
    is_ctx = i < n_ctx_blocks
    seq_start = is_ctx | ((i - n_ctx_blocks) % per_seq == 0)

    @pl.when(seq_start)
    def _():
        s_ref[...] = jnp.where(is_ctx, 0.0, s0_ref[...])

    views = ((qf_ref, kf_ref, vf_ref, gf_ref), (qb_ref, kb_ref, vb_ref, gb_ref))
    qs, ks, vs, gs, bs, grs = [], [], [], [], [], []
    for d, (q_ref, k_ref, v_ref, g_ref) in enumerate(views):
        gates_all = g_ref[...]
        gates_t = jnp.transpose(gates_all)
        for ci in range(nc):
            rows = slice(ci * c, (ci + 1) * c)
            gates = gates_all[rows]
            for h in range(nh):
                lanes = slice(h * DN_DK, (h + 1) * DN_DK)
                lane = d * nh + h
                qs.append(q_ref[rows, lanes])
                ks.append(k_ref[rows, lanes])
                vs.append(v_ref[rows, lanes])
                gs.append(_lane_broadcast(gates, lane))
                bs.append(_lane_broadcast(gates, 2 * nh + lane))
                grs.append(gates_t[lane:lane + 1, rows])
    q, k, v = jnp.stack(qs), jnp.stack(ks), jnp.stack(vs)
    g, beta, g_row = jnp.stack(gs), jnp.stack(bs), jnp.stack(grs)
    half = nc * nh

    row, col, base_mask, level_masks = _chunk_masks()
    qk_kk = _bmm_nt(jnp.concatenate([q, k], axis=1), k)
    decay = jnp.exp(_dir_where(row >= col, row <= col, g[:, :, :c] - g_row, -jnp.inf))
    a = beta[:, :, :c] * qk_kk[:, c:] * _dir_where(row > col, row < col, decay, 0.0)
    p = qk_kk[:, :c] * decay
    n = _unit_tri_inverse_minus_eye(a, base_mask, level_masks)
    eg = jnp.exp(g)
    g_last = jnp.concatenate([g[:half, c - 1:c], g[half:, 0:1]], axis=0)
    rhs = jnp.concatenate([beta * v, (beta * eg) * k], axis=2)
    sol = rhs + _bmm(n, rhs)
    u_t = sol[:, :, :DN_DK]
    wq = jnp.concatenate([sol[:, :, DN_DK:], q * eg], axis=1)
    kd = k * jnp.exp(g_last - g)
    gl = jnp.exp(g_last)

    s = s_ref[...].reshape(2 * nh, DN_DK, DN_DK)
    for ci in range(nc):
        cb = nc - 1 - ci
        pick = lambda x: jnp.concatenate([x[ci * nh:(ci + 1) * nh],
                                          x[half + cb * nh:half + (cb + 1) * nh]], axis=0)
        ws = _bmm(pick(wq), s)
        u = pick(u_t) - ws[:, :c]
        o = ws[:, c:] + _bmm(pick(p), u)
        s = pick(gl) * s + _bmm_tn(pick(kd), u)
        for h in range(nh):
            lanes = slice(h * DN_DK, (h + 1) * DN_DK)
            of_ref[ci * c:(ci + 1) * c, lanes] = o[h]
            ob_ref[cb * c:(cb + 1) * c, lanes] = o[nh + h]
    s = s.reshape(2, nh, DN_DK, DN_DK)
    s_ref[...] = s
    sout_ref[...] = s


def _delta_rule(lay, qkv, gates, s0, layer_j):
    n_blocks = lay.n_tok // SEQ_BLOCK
    n_ctx_blocks = lay.n_ctx_tok // SEQ_BLOCK
    per_seq = lay.dec_len // SEQ_BLOCK
    vd = DN_HEADS * DN_DK

    def rev(i):
        j = i - n_ctx_blocks
        return jnp.where(i < n_ctx_blocks, i, n_ctx_blocks + (j // per_seq) * per_seq + per_seq - 1 - j % per_seq)

    fwd = lambda i: i
    cols = lambda blk, part: pl.BlockSpec((SEQ_BLOCK, vd), lambda i: (blk(i), part))
    gate_spec = lambda blk: pl.BlockSpec((SEQ_BLOCK, LANES), lambda i: (blk(i), 0))
    state_shape = (2, DN_HEADS, DN_DK, DN_DK)
    s0_spec = pl.BlockSpec((None, None) + state_shape,
                           lambda i: (jnp.maximum(i - n_ctx_blocks, 0) // per_seq, layer_j, 0, 0, 0, 0))
    sout_spec = pl.BlockSpec((None,) + state_shape,
                             lambda i: (jnp.minimum(i, n_ctx_blocks), 0, 0, 0, 0))
    o_shape = jax.ShapeDtypeStruct((lay.n_tok, vd), F32)
    return pl.pallas_call(
        functools.partial(_delta_kernel, lay=lay),
        out_shape=(o_shape, o_shape,
                   jax.ShapeDtypeStruct((n_ctx_blocks + 1,) + state_shape, F32)),
        grid=(n_blocks,),
        in_specs=[cols(fwd, 0), cols(fwd, 1), cols(fwd, 2), gate_spec(fwd),
                  cols(rev, 0), cols(rev, 1), cols(rev, 2), gate_spec(rev), s0_spec],
        out_specs=(cols(fwd, 0), cols(rev, 0), sout_spec),
        scratch_shapes=[pltpu.VMEM(state_shape, F32)],
        compiler_params=_cparams(1),
        name="delta_rule",
    )(qkv, qkv, qkv, gates, qkv, qkv, qkv, gates, s0)


def _dn_out_kernel(x_ref, of_ref, ob_ref, z_ref, mod_ref, ng_ref, w_ref, g_ref, b_ref, o_ref, act_ref,
                   *, alpha):
    for h in range(DN_HEADS):
        lanes = slice(h * DN_DK, (h + 1) * DN_DK)
        o = of_ref[:, lanes] + ob_ref[:, lanes]
        rms = lax.rsqrt(jnp.mean(o * o, axis=-1, keepdims=True) + LN_EPS)
        act_ref[:, lanes] = (o * rms * ng_ref[...] * _silu(z_ref[:, lanes])).astype(BF16)
    y = jnp.dot(act_ref[...], w_ref[...], preferred_element_type=F32)
    o_ref[...] = _residual_norm(x_ref[...], y, 1.0, mod_ref, 1, g_ref, b_ref, alpha)


def _dn_out(lay, x, o_f, o_b, proj, mod, layer, norm_g, w_out, g, b, alpha):
    d = lay.d
    vd = DN_HEADS * DN_DK
    z_col = (proj.shape[1] - vd) // vd
    return pl.pallas_call(
        functools.partial(_dn_out_kernel, alpha=alpha),
        out_shape=jax.ShapeDtypeStruct((lay.n_tok, d), F32),
        grid=(lay.n_tok // TOK_TILE,),
        in_specs=[_row_spec(TOK_TILE, d), _row_spec(TOK_TILE, vd), _row_spec(TOK_TILE, vd),
                  _row_spec(TOK_TILE, vd, z_col), lay.mod_spec(layer, TOK_TILE),
                  _const_spec((1, DN_DK)), _const_spec((vd, d)), _const_spec((1, d)), _const_spec((1, d))],
        out_specs=_row_spec(TOK_TILE, d),
        scratch_shapes=[pltpu.VMEM((TOK_TILE, vd), BF16)],
        compiler_params=_cparams(1),
        name="dn_out",
    )(x, o_f, o_b, proj, mod, norm_g.reshape(1, DN_DK), w_out, g.reshape(1, d), b.reshape(1, d))


def _dn_mixer(lay, x, mod, layer, w_in, conv_w, a_log, dt_bias, norm_g, w_out, s0, layer_j, g, b, alpha):
    qk, vd = DN_HEADS * DN_DK, DN_HEADS * DN_DK
    n_qkv = 2 * qk + vd
    n_gate = 4 * DN_HEADS
    assert w_in.shape[1] == n_qkv + vd + n_gate and n_gate <= LANES
    w_qkvz = w_in[:, :n_qkv + vd].astype(BF16)
    w_ab = jnp.pad(w_in[:, n_qkv + vd:], ((0, 0), (0, LANES - n_gate))).astype(BF16)
    lane_row = lambda v: jnp.pad(v.reshape(1, 2 * DN_HEADS), ((0, 0), (0, LANES - 2 * DN_HEADS)))
    proj, gates = _dn_in(lay, x, mod, layer, w_qkvz, w_ab, lane_row(a_log), lane_row(dt_bias))
    conv_pad = jnp.pad(conv_w, ((0, 8 - conv_w.shape[0]), (0, 0)))
    qkv = _dn_conv(lay, proj, conv_pad, n_qkv)
    o_f, o_b, s_fin = _delta_rule(lay, qkv, gates, s0, layer_j)
    x = _dn_out(lay, x, o_f, o_b, proj, mod, layer, norm_g, w_out, g, b, alpha)
    return x, s_fin[:lay.n_ctx_seq]


def kernel(x_prompt, x_sample, state_delta, c, c_ctx, w_mod, b_mod, ln_g, ln_b, ffn_w_in, ffn_w_out,
           cv_w1, cv_b1, cv_dw, cv_dwb, cv_ln_g, cv_ln_b, cv_w2, cv_b2,
           dn_w_in, dn_conv, dn_a_log, dn_dt_bias, dn_norm_g, dn_w_out):
    n_ctx_seq, ctx_len, d = x_prompt.shape
    n_dec_seq, dec_len, _ = x_sample.shape
    depth = w_mod.shape[0]
    alpha = (2.0 * depth) ** 0.25
    lay = _Layout(n_ctx_seq, ctx_len, n_dec_seq, dec_len, d)

    x = jnp.concatenate([x_prompt.reshape(lay.n_ctx_tok, d), x_sample.reshape(lay.n_dec_tok, d)], axis=0)
    cond = jnp.concatenate([c_ctx[None, :], c, jnp.zeros((COND_ROWS - 1 - n_dec_seq, d), F32)], axis=0)
    mod = _adaln(cond, w_mod, b_mod).reshape(depth, COND_ROWS, N_MOD, d)

    states = []
    for l in range(depth):
        x = _ffn(lay, x, mod, l, 0, ffn_w_in[l, 0].astype(BF16), ffn_w_out[l, 0].astype(BF16),
                 ln_g[l, 0], ln_b[l, 0], alpha)
        j = l // N_MIXERS
        if l % N_MIXERS == 0:
            x = _conv_mixer(lay, x, mod, l, cv_w1[j].astype(BF16), cv_b1[j], cv_dw[j], cv_dwb[j],
                            cv_ln_g[j], cv_ln_b[j], cv_w2[j].astype(BF16), cv_b2[j],
                            ln_g[l, 1], ln_b[l, 1], alpha)
        else:
            x, s_fin = _dn_mixer(lay, x, mod, l, dn_w_in[j], dn_conv[j], dn_a_log[j], dn_dt_bias[j],
                                 dn_norm_g[j], dn_w_out[j].astype(BF16), state_delta, j,
                                 ln_g[l, 1], ln_b[l, 1], alpha)
            states.append(s_fin)
        x = _ffn(lay, x, mod, l, 2, ffn_w_in[l, 1].astype(BF16), ffn_w_out[l, 1].astype(BF16),
                 ln_g[l, 2], ln_b[l, 2], alpha)

    y_prompt = x[:lay.n_ctx_tok].reshape(n_ctx_seq, ctx_len, d)
    y_sample = x[lay.n_ctx_tok:].reshape(n_dec_seq, dec_len, d)
    new_state = jnp.stack(states, axis=1).astype(state_delta.dtype)
    return y_prompt, y_sample, new_state
```

```python
import functools

import jax
import jax.numpy as jnp
from jax import lax
from jax.experimental import pallas as pl
from jax.experimental.pallas import tpu as pltpu

F32 = jnp.float32
BF16 = jnp.bfloat16

LN_EPS = 1e-5
L2_EPS = 1e-6
N_MOD = 9
N_MIXERS = 2
GRID_W = 64
DN_HEADS = 8
DN_DK = 128
DN_CHUNK = 64
COND_ROWS = 8
LANES = 128
VMEM_LIMIT = 56 * 1024 * 1024

TOK_TILE = 512
SEQ_BLOCK = 256
FF_CHUNK = 256


def _cparams(n_axes):
    return pltpu.CompilerParams(dimension_semantics=("arbitrary",) * n_axes,
                                vmem_limit_bytes=VMEM_LIMIT)


def _mm(a, b):
    return jnp.dot(a.astype(BF16), b.astype(BF16), preferred_element_type=F32)


def _mm_nt(a, b):
    return lax.dot_general(a.astype(BF16), b.astype(BF16), (((1,), (1,)), ((), ())),
                           preferred_element_type=F32)


def _mm_tn(a, b):
    return lax.dot_general(a.astype(BF16), b.astype(BF16), (((0,), (0,)), ((), ())),
                           preferred_element_type=F32)


def _sigmoid(x):
    return 1.0 / (1.0 + jnp.exp(-x))


def _silu(x):
    return x * _sigmoid(x)


def _layer_norm(r, g, b):
    mu = jnp.mean(r, axis=-1, keepdims=True)
    rc = r - mu
    var = jnp.mean(rc * rc, axis=-1, keepdims=True)
    return rc * lax.rsqrt(var + LN_EPS) * g + b


def _modulate(x, mod_ref, slot):
    shift = mod_ref[3 * slot:3 * slot + 1, :]
    scale = mod_ref[3 * slot + 1:3 * slot + 2, :]
    return x * (1.0 + scale) + shift


def _residual_norm(x, y, gate_scale, mod_ref, slot, g_ref, b_ref, alpha):
    gate = mod_ref[3 * slot + 2:3 * slot + 3, :]
    return _layer_norm(alpha * x + (gate_scale * gate) * y, g_ref[...], b_ref[...])


class _Layout:
    def __init__(self, n_ctx_seq, ctx_len, n_dec_seq, dec_len, d_model):
        self.n_ctx_seq, self.ctx_len = n_ctx_seq, ctx_len
        self.n_dec_seq, self.dec_len = n_dec_seq, dec_len
        self.n_ctx_tok = n_ctx_seq * ctx_len
        self.n_dec_tok = n_dec_seq * dec_len
        self.n_tok = self.n_ctx_tok + self.n_dec_tok
        self.d = d_model
        assert n_dec_seq + 1 <= COND_ROWS
        assert self.n_ctx_tok % TOK_TILE == 0 and dec_len % TOK_TILE == 0
        assert ctx_len == SEQ_BLOCK and dec_len % SEQ_BLOCK == 0
        assert dec_len % GRID_W == 0 and SEQ_BLOCK % GRID_W == 0

    def mod_row(self, i, tile):
        n_ctx_tiles = self.n_ctx_tok // tile
        per_seq = self.dec_len // tile
        return jnp.where(i < n_ctx_tiles, 0, 1 + (i - n_ctx_tiles) // per_seq)

    def mod_spec(self, layer, tile):
        return pl.BlockSpec((None, None, N_MOD, self.d),
                            lambda i: (layer, self.mod_row(i, tile), 0, 0))

    def group_specs(self, tile, width):
        n_ctx_tiles = self.n_ctx_tok // tile
        return (pl.BlockSpec((tile, width), lambda i: (jnp.minimum(i, n_ctx_tiles - 1), 0)),
                pl.BlockSpec((tile, width), lambda i: (jnp.maximum(i - n_ctx_tiles, 0), 0)))


def _row_spec(tile, width, col=0):
    return pl.BlockSpec((tile, width), lambda i: (i, col))


def _const_spec(shape):
    return pl.BlockSpec(shape, lambda *_: (0,) * len(shape))


def _adaln_kernel(cond_ref, w_ref, b_ref, o_ref):
    cnd = cond_ref[...]
    o_ref[...] = jnp.dot(_silu(cnd), w_ref[...], preferred_element_type=F32,
                         precision=lax.Precision.HIGHEST) + b_ref[...]


def _adaln(cond, w_mod, b_mod):
    depth, d, n_out = w_mod.shape
    bn = n_out // 4
    return pl.pallas_call(
        _adaln_kernel,
        out_shape=jax.ShapeDtypeStruct((depth, COND_ROWS, n_out), F32),
        grid=(depth, n_out // bn),
        in_specs=[pl.BlockSpec((COND_ROWS, d), lambda l, j: (0, 0)),
                  pl.BlockSpec((None, d, bn), lambda l, j: (l, 0, j)),
                  pl.BlockSpec((None, 1, bn), lambda l, j: (l, 0, j))],
        out_specs=pl.BlockSpec((None, COND_ROWS, bn), lambda l, j: (l, 0, j)),
        compiler_params=_cparams(2),
        name="adaln",
    )(cond, w_mod, b_mod.reshape(depth, 1, n_out))


def _ffn_kernel(*refs, slot, d_ff, alpha, n_ctx_tiles):
    if n_ctx_tiles is None:
        x_ref, mod_ref, w_in_ref, w_out_ref, g_ref, b_ref, o_ref, act_ref = refs
        x = x_ref[...]
    else:
        xc_ref, xd_ref, mod_ref, w_in_ref, w_out_ref, g_ref, b_ref, o_ref, act_ref = refs
        x = jnp.where(pl.program_id(0) < n_ctx_tiles, xc_ref[...], xd_ref[...])
    h = _modulate(x, mod_ref, slot).astype(BF16)
    for j in range(d_ff // FF_CHUNK):
        lo = j * FF_CHUNK
        lin = jnp.dot(h, w_in_ref[:, lo:lo + FF_CHUNK], preferred_element_type=F32)
        gat = jnp.dot(h, w_in_ref[:, d_ff + lo:d_ff + lo + FF_CHUNK], preferred_element_type=F32)
        act_ref[:, lo:lo + FF_CHUNK] = (_silu(gat) * lin).astype(BF16)
    y = jnp.dot(act_ref[...], w_out_ref[...], preferred_element_type=F32)
    o_ref[...] = _residual_norm(x, y, 0.5, mod_ref, slot, g_ref, b_ref, alpha)


def _ffn(lay, xs, mod, layer, slot, w_in, w_out, g, b, alpha):
    d, d_ff = lay.d, w_out.shape[0]
    assert d_ff % FF_CHUNK == 0
    split = isinstance(xs, tuple)
    x_specs = list(lay.group_specs(TOK_TILE, d)) if split else [_row_spec(TOK_TILE, d)]
    return pl.pallas_call(
        functools.partial(_ffn_kernel, slot=slot, d_ff=d_ff, alpha=alpha,
                          n_ctx_tiles=lay.n_ctx_tok // TOK_TILE if split else None),
        out_shape=jax.ShapeDtypeStruct((lay.n_tok, d), F32),
        grid=(lay.n_tok // TOK_TILE,),
        in_specs=x_specs + [lay.mod_spec(layer, TOK_TILE),
                            _const_spec((d, 2 * d_ff)), _const_spec((d_ff, d)),
                            _const_spec((1, d)), _const_spec((1, d))],
        out_specs=_row_spec(TOK_TILE, d),
        scratch_shapes=[pltpu.VMEM((TOK_TILE, d_ff), BF16)],
        compiler_params=_cparams(1),
        name="ffn",
    )(*(xs if split else (xs,)), mod, w_in, w_out, g.reshape(1, d), b.reshape(1, d))


def _conv_in_kernel(x_ref, mod_ref, w1_ref, b1_ref, u_ref, *, d):
    h = _modulate(x_ref[...], mod_ref, 1).astype(BF16)
    lin = jnp.dot(h, w1_ref[:, :d], preferred_element_type=F32) + b1_ref[:, :d]
    gat = jnp.dot(h, w1_ref[:, d:], preferred_element_type=F32) + b1_ref[:, d:]
    u_ref[...] = lin * _sigmoid(gat)


def _conv_in(lay, x, mod, layer, w1, b1):
    d = lay.d
    return pl.pallas_call(
        functools.partial(_conv_in_kernel, d=d),
        out_shape=jax.ShapeDtypeStruct((lay.n_tok, d), F32),
        grid=(lay.n_tok // TOK_TILE,),
        in_specs=[_row_spec(TOK_TILE, d), lay.mod_spec(layer, TOK_TILE),
                  _const_spec((d, 2 * d)), _const_spec((1, 2 * d))],
        out_specs=_row_spec(TOK_TILE, d),
        compiler_params=_cparams(1),
        name="conv_in",
    )(x, mod, w1, b1.reshape(1, 2 * d))


CONV_PAD = 16
CONV_ROWS = 64


SUBLANES = 8
DEC_CONV_CH = 256


def _segment_conv(u_ref, w_ref, o_ref, pad_ref, sh_ref, *, taps, seg_len, n_seg):
    n_ch = u_ref.shape[1]
    half = taps // 2
    n_sh = sh_ref.shape[1]
    zeros = jnp.zeros((CONV_PAD, n_ch), F32)
    pad_ref[0:CONV_PAD, :] = zeros
    pad_ref[CONV_PAD + seg_len:2 * CONV_PAD + seg_len, :] = zeros

    def segment(s, carry):
        row0 = pl.multiple_of(s * seg_len, seg_len)
        pad_ref[CONV_PAD:CONV_PAD + seg_len, :] = u_ref[pl.ds(row0, seg_len), :]

        def lane_tile(j, carry):
            lanes = pl.ds(pl.multiple_of(j * LANES, LANES), LANES)
            for p in range(1, SUBLANES):
                sh_ref[p] = pad_ref[p:p + n_sh, lanes]
            for r in range(seg_len // CONV_ROWS):
                acc = jnp.zeros((CONV_ROWS, LANES), F32)
                for k in range(taps):
                    off = CONV_PAD - half + k
                    p = off % SUBLANES
                    start = off - p + r * CONV_ROWS
                    src = (pad_ref[start:start + CONV_ROWS, lanes] if p == 0
                           else sh_ref[p, start:start + CONV_ROWS, :])
                    acc = acc + w_ref[k:k + 1, lanes] * src
                o_ref[pl.ds(row0 + r * CONV_ROWS, CONV_ROWS), lanes] = acc
            return carry

        return lax.fori_loop(0, n_ch // LANES, lane_tile, carry)

    lax.fori_loop(0, n_seg, segment, 0)


def _segment_conv_scratch(seg_len, n_ch):
    return [pltpu.VMEM((seg_len + 2 * CONV_PAD, n_ch), F32),
            pltpu.VMEM((SUBLANES, seg_len + 2 * CONV_PAD - SUBLANES, LANES), F32)]


def _ctx_conv_kernel(u_ref, w_ref, o_ref, pad_ref, sh_ref, *, taps):
    _segment_conv(u_ref, w_ref, o_ref, pad_ref, sh_ref, taps=taps, seg_len=u_ref.shape[0], n_seg=1)


def _ctx_conv(lay, u, w, taps):
    d = lay.d
    return pl.pallas_call(
        functools.partial(_ctx_conv_kernel, taps=taps),
        out_shape=jax.ShapeDtypeStruct((lay.n_ctx_tok, d), F32),
        grid=(lay.n_ctx_seq,),
        in_specs=[pl.BlockSpec((lay.ctx_len, d), lambda i: (i, 0)),
                  pl.BlockSpec((w.shape[0], d), lambda i: (0, 0))],
        out_specs=pl.BlockSpec((lay.ctx_len, d), lambda i: (i, 0)),
        scratch_shapes=_segment_conv_scratch(lay.ctx_len, d),
        compiler_params=_cparams(1),
        name="ctx_conv",
    )(u, w)


def _dec_conv_kernel(u_ref, w_ref, o_ref, pad_ref, sh_ref, vpad_ref, *, taps, n_row_blocks):
    j = pl.program_id(1)
    seq_len, n_ch = u_ref.shape

    @pl.when(j < n_row_blocks)
    def _():
        _segment_conv(u_ref, w_ref, o_ref, pad_ref, sh_ref, taps=taps, seg_len=GRID_W,
                      n_seg=seq_len // GRID_W)

    @pl.when(j >= n_row_blocks)
    def _():
        halo = (taps // 2) * GRID_W
        zeros = jnp.zeros((halo, n_ch), F32)
        vpad_ref[0:halo, :] = zeros
        vpad_ref[halo + seq_len:2 * halo + seq_len, :] = zeros
        vpad_ref[halo:halo + seq_len, :] = u_ref[...]

        def row_chunk(r, carry):
            base = pl.multiple_of(r * CONV_ROWS, CONV_ROWS)
            for t in range(n_ch // LANES):
                lanes = slice(t * LANES, (t + 1) * LANES)
                acc = jnp.zeros((CONV_ROWS, LANES), F32)
                for k in range(taps):
                    acc = acc + w_ref[k:k + 1, lanes] * vpad_ref[pl.ds(base + k * GRID_W, CONV_ROWS), lanes]
                o_ref[pl.ds(base, CONV_ROWS), lanes] = acc
            return carry

        lax.fori_loop(0, seq_len // CONV_ROWS, row_chunk, 0)


def _dec_conv(lay, u, w, taps):
    d = lay.d
    first_seq = lay.n_ctx_tok // lay.dec_len
    assert lay.n_ctx_tok % lay.dec_len == 0 and (d // 2) % DEC_CONV_CH == 0
    halo = (taps // 2) * GRID_W
    return pl.pallas_call(
        functools.partial(_dec_conv_kernel, taps=taps, n_row_blocks=(d // 2) // DEC_CONV_CH),
        out_shape=jax.ShapeDtypeStruct((lay.n_dec_tok, d), F32),
        grid=(lay.n_dec_seq, d // DEC_CONV_CH),
        in_specs=[pl.BlockSpec((lay.dec_len, DEC_CONV_CH), lambda b, j: (first_seq + b, j)),
                  pl.BlockSpec((w.shape[0], DEC_CONV_CH), lambda b, j: (0, j))],
        out_specs=pl.BlockSpec((lay.dec_len, DEC_CONV_CH), lambda b, j: (b, j)),
        scratch_shapes=_segment_conv_scratch(GRID_W, DEC_CONV_CH)
        + [pltpu.VMEM((lay.dec_len + 2 * halo, DEC_CONV_CH), F32)],
        compiler_params=_cparams(2),
        name="dec_conv",
    )(u, w)


def _conv_out_kernel(x_ref, cvc_ref, cvd_ref, mod_ref, dwb_ref, lng_ref, lnb_ref, w2_ref, b2_ref, g_ref, b_ref,
                     o_ref, *, alpha, n_ctx_tiles):
    cv = jnp.where(pl.program_id(0) < n_ctx_tiles, cvc_ref[...], cvd_ref[...])
    a = _silu(_layer_norm(cv + dwb_ref[...], lng_ref[...], lnb_ref[...]))
    y = jnp.dot(a.astype(BF16), w2_ref[...], preferred_element_type=F32) + b2_ref[...]
    o_ref[...] = _residual_norm(x_ref[...], y, 1.0, mod_ref, 1, g_ref, b_ref, alpha)


def _conv_out(lay, x, cv_ctx, cv_dec, mod, layer, dwb, lng, lnb, w2, b2, g, b, alpha):
    d = lay.d
    vec = lambda v: v.reshape(1, d)
    return pl.pallas_call(
        functools.partial(_conv_out_kernel, alpha=alpha, n_ctx_tiles=lay.n_ctx_tok // TOK_TILE),
        out_shape=jax.ShapeDtypeStruct((lay.n_tok, d), F32),
        grid=(lay.n_tok // TOK_TILE,),
        in_specs=[_row_spec(TOK_TILE, d), *lay.group_specs(TOK_TILE, d), lay.mod_spec(layer, TOK_TILE),
                  _const_spec((1, d)), _const_spec((1, d)), _const_spec((1, d)),
                  _const_spec((d, d)), _const_spec((1, d)), _const_spec((1, d)), _const_spec((1, d))],
        out_specs=_row_spec(TOK_TILE, d),
        compiler_params=_cparams(1),
        name="conv_out",
    )(x, cv_ctx, cv_dec, mod, vec(dwb), vec(lng), vec(lnb), w2, vec(b2), vec(g), vec(b))


def _conv_mixer(lay, x, mod, layer, w1, b1, dw, dwb, lng, lnb, w2, b2, g, b, alpha):
    taps = dw.shape[0]
    assert taps // 2 < CONV_PAD
    u = _conv_in(lay, x, mod, layer, w1, b1)
    dw_pad = jnp.pad(dw, ((0, -taps % SUBLANES), (0, 0)))
    cv_ctx = _ctx_conv(lay, u, dw_pad, taps)
    cv_dec = _dec_conv(lay, u, dw_pad, taps)
    return _conv_out(lay, x, cv_ctx, cv_dec, mod, layer, dwb, lng, lnb, w2, b2, g, b, alpha)


def _split3(x):
    hi = x.astype(BF16)
    r1 = x - hi.astype(F32)
    mid = r1.astype(BF16)
    lo = (r1 - mid.astype(F32)).astype(BF16)
    return hi, mid, lo


def _chunk_cumsum(tri, x):
    hi, mid, lo = _split3(x)
    dot = lambda t: jnp.dot(tri, t, preferred_element_type=F32)
    return dot(hi) + dot(mid) + dot(lo)


DN_HALO = 16
DN_COLS = 512


def _dn_in_kernel(x_ref, xp_ref, xn_ref, mod_ref, w_ref, wab_ref, cw_ref, alog_ref, dtb_ref,
                  qkv_ref, z_ref, gates_ref, h_ref, pre_ref, *, lay, n_qkv):
    i = pl.program_id(0)
    n_ctx_blocks = lay.n_ctx_tok // SEQ_BLOCK
    per_seq = lay.dec_len // SEQ_BLOCK
    pos = (i - n_ctx_blocks) % per_seq
    is_ctx = i < n_ctx_blocks
    has_prev = jnp.logical_not(is_ctx | (pos == 0))
    has_next = jnp.logical_not(is_ctx | (pos == per_seq - 1))
    lo, hi = DN_HALO, DN_HALO + SEQ_BLOCK
    h_ref[0:lo] = jnp.where(has_prev, _modulate(xp_ref[...], mod_ref, 1), 0.0).astype(BF16)
    h_ref[lo:hi] = _modulate(x_ref[...], mod_ref, 1).astype(BF16)
    h_ref[hi:hi + DN_HALO] = jnp.where(has_next, _modulate(xn_ref[...], mod_ref, 1), 0.0).astype(BF16)

    n_qk = 2 * DN_HEADS * DN_DK
    for j in range(n_qkv // DN_COLS):
        cols = slice(j * DN_COLS, (j + 1) * DN_COLS)
        pre_ref[:, cols] = jnp.dot(h_ref[...], w_ref[:, cols], preferred_element_type=F32)
        y = _silu(cw_ref[0:1, cols] * pre_ref[lo - 1:hi - 1, cols] + cw_ref[1:2, cols] * pre_ref[lo:hi, cols]
                  + cw_ref[2:3, cols] * pre_ref[lo + 1:hi + 1, cols])
        for t in range(DN_COLS // DN_DK):
            c0 = j * DN_COLS + t * DN_DK
            yt = y[:, t * DN_DK:(t + 1) * DN_DK]
            if c0 < n_qk:
                inv = lax.rsqrt(jnp.sum(yt * yt, axis=-1, keepdims=True) + L2_EPS)
                yt = yt * (inv * (DN_DK ** -0.5 if c0 < n_qk // 2 else 1.0))
            qkv_ref[:, c0:c0 + DN_DK] = yt

    h = h_ref[lo:hi]
    z_ref[...] = jnp.dot(h, w_ref[:, n_qkv:], preferred_element_type=F32)
    ab = jnp.dot(h, wab_ref[...], preferred_element_type=F32)
    pre = ab + dtb_ref[...]
    softplus = jnp.maximum(pre, 0.0) + jnp.log(1.0 + jnp.exp(-jnp.abs(pre)))
    log_a = -jnp.exp(alog_ref[...]) * softplus
    beta = _sigmoid(ab)
    tile = ab.shape[0]
    row = lax.broadcasted_iota(jnp.int32, (tile, tile), 0)
    col = lax.broadcasted_iota(jnp.int32, (tile, tile), 1)
    same = (row // DN_CHUNK) == (col // DN_CHUNK)
    lower = (same & (row >= col)).astype(BF16)
    upper = (same & (row <= col)).astype(BF16)
    lane = lax.broadcasted_iota(jnp.int32, ab.shape, 1)
    g_fwd = _chunk_cumsum(lower, log_a)
    g_bwd = _chunk_cumsum(upper, log_a)
    gates_ref[...] = jnp.where(lane < DN_HEADS, g_fwd, jnp.where(lane < 2 * DN_HEADS, g_bwd, beta))


def _dn_in(lay, x, mod, layer, w_qkvz, w_ab, conv_w, alog_row, dtb_row, n_qkv):
    d, n_proj = w_qkvz.shape
    assert n_qkv % DN_COLS == 0 and SEQ_BLOCK % DN_HALO == 0
    sub = SEQ_BLOCK // DN_HALO
    n_units = lay.n_tok // DN_HALO
    return pl.pallas_call(
        functools.partial(_dn_in_kernel, lay=lay, n_qkv=n_qkv),
        out_shape=(jax.ShapeDtypeStruct((lay.n_tok, n_qkv), F32),
                   jax.ShapeDtypeStruct((lay.n_tok, n_proj - n_qkv), F32),
                   jax.ShapeDtypeStruct((lay.n_tok, LANES), F32)),
        grid=(lay.n_tok // SEQ_BLOCK,),
        in_specs=[_row_spec(SEQ_BLOCK, d),
                  pl.BlockSpec((DN_HALO, d), lambda i: (jnp.maximum(i * sub - 1, 0), 0)),
                  pl.BlockSpec((DN_HALO, d), lambda i: (jnp.minimum((i + 1) * sub, n_units - 1), 0)),
                  lay.mod_spec(layer, SEQ_BLOCK),
                  _const_spec((d, n_proj)), _const_spec((d, LANES)), _const_spec((SUBLANES, n_qkv)),
                  _const_spec((1, LANES)), _const_spec((1, LANES))],
        out_specs=(_row_spec(SEQ_BLOCK, n_qkv), _row_spec(SEQ_BLOCK, n_proj - n_qkv),
                   _row_spec(SEQ_BLOCK, LANES)),
        scratch_shapes=[pltpu.VMEM((SEQ_BLOCK + 2 * DN_HALO, d), BF16),
                        pltpu.VMEM((SEQ_BLOCK + 2 * DN_HALO, n_qkv), F32)],
        compiler_params=_cparams(1),
        name="dn_in",
    )(x, x, x, mod, w_qkvz, w_ab, conv_w, alog_row, dtb_row)


def _bmm(a, b):
    return jnp.einsum("bmk,bkn->bmn", a.astype(BF16), b.astype(BF16), preferred_element_type=F32)


def _bmm_nt(a, b):
    return jnp.einsum("bmk,bnk->bmn", a.astype(BF16), b.astype(BF16), preferred_element_type=F32)


def _bmm_tn(a, b):
    return jnp.einsum("bkm,bkn->bmn", a.astype(BF16), b.astype(BF16), preferred_element_type=F32)


def _chunk_masks():
    c = DN_CHUNK
    row = lax.broadcasted_iota(jnp.int32, (c, c), 0)
    col = lax.broadcasted_iota(jnp.int32, (c, c), 1)
    same = lambda n: (row // n) == (col // n)
    levels = []
    n = 8
    while n < c:
        levels.append(same(2 * n) & jnp.logical_not(same(n)))
        n *= 2
    return row, col, same(8), levels


def _dir_where(mask_fwd, mask_bwd, x, other):
    half = x.shape[0] // 2
    return jnp.concatenate([jnp.where(mask_fwd[None], x[:half], other),
                            jnp.where(mask_bwd[None], x[half:], other)], axis=0)


def _unit_tri_inverse_minus_eye(a, base_mask, level_masks):
    d = jnp.where(base_mask[None], a, 0.0)
    x = _bmm(d, d)
    x2 = _bmm(x, x)
    m1 = x - d - _bmm(d, x)
    n = m1 + x2 + _bmm(m1, x2)
    for mask in level_masks:
        e = jnp.where(mask[None], a, 0.0)
        y = e + _bmm(n, e)
        n = n - y - _bmm(y, n)
    return n


def _lane_broadcast(x, lane):
    lanes = lax.broadcasted_iota(jnp.int32, x.shape, 1)
    col = jnp.sum(jnp.where(lanes == lane, x, 0.0), axis=-1, keepdims=True)
    return jnp.broadcast_to(col, x.shape)


def _delta_kernel(qf_ref, kf_ref, vf_ref, gf_ref, qb_ref, kb_ref, vb_ref, gb_ref, s0_ref,
                  of_ref, ob_ref, sout_ref, s_ref, *, lay):
    i = pl.program_id(0)
    n_ctx_blocks = lay.n_ctx_tok // SEQ_BLOCK
    per_seq = lay.dec_len // SEQ_BLOCK
    nc, nh, c = SEQ_BLOCK // DN_CHUNK, DN_HEADS, DN_CHUNK
    is_ctx = i < n_ctx_blocks
    seq_start = is_ctx | ((i - n_ctx_blocks) % per_seq == 0)

    @pl.when(seq_start)
    def _():
        s_ref[...] = jnp.where(is_ctx, 0.0, s0_ref[...])

    views = ((qf_ref, kf_ref, vf_ref, gf_ref), (qb_ref, kb_ref, vb_ref, gb_ref))
    qs, ks, vs, gs, bs, grs = [], [], [], [], [], []
    for d, (q_ref, k_ref, v_ref, g_ref) in enumerate(views):
        gates_all = g_ref[...]
        gates_t = jnp.transpose(gates_all)
        for ci in range(nc):
            rows = slice(ci * c, (ci + 1) * c)
            gates = gates_all[rows]
            for h in range(nh):
                lanes = slice(h * DN_DK, (h + 1) * DN_DK)
                lane = d * nh + h
                qs.append(q_ref[rows, lanes])
                ks.append(k_ref[rows, lanes])
                vs.append(v_ref[rows, lanes])
                gs.append(_lane_broadcast(gates, lane))
                bs.append(_lane_broadcast(gates, 2 * nh + lane))
                grs.append(gates_t[lane:lane + 1, rows])
    q, k, v = jnp.stack(qs), jnp.stack(ks), jnp.stack(vs)
    g, beta, g_row = jnp.stack(gs), jnp.stack(bs), jnp.stack(grs)
    half = nc * nh

    row, col, base_mask, level_masks = _chunk_masks()
    qk_kk = _bmm_nt(jnp.concatenate([q, k], axis=1), k)
    decay = jnp.exp(_dir_where(row >= col, row <= col, g[:, :, :c] - g_row, -jnp.inf))
    a = beta[:, :, :c] * qk_kk[:, c:] * _dir_where(row > col, row < col, decay, 0.0)
    p = qk_kk[:, :c] * decay
    n = _unit_tri_inverse_minus_eye(a, base_mask, level_masks)
    eg = jnp.exp(g)
    g_last = jnp.concatenate([g[:half, c - 1:c], g[half:, 0:1]], axis=0)
    rhs = jnp.concatenate([beta * v, (beta * eg) * k], axis=2)
    sol = rhs + _bmm(n, rhs)
    u_t = sol[:, :, :DN_DK]
    wq = jnp.concatenate([sol[:, :, DN_DK:], q * eg], axis=1)
    kd = k * jnp.exp(g_last - g)
    gl = jnp.exp(g_last)

    s = s_ref[...].reshape(2 * nh, DN_DK, DN_DK)
    for ci in range(nc):
        cb = nc - 1 - ci
        pick = lambda x: jnp.concatenate([x[ci * nh:(ci + 1) * nh],
                                          x[half + cb * nh:half + (cb + 1) * nh]], axis=0)
        ws = _bmm(pick(wq), s)
        u = pick(u_t) - ws[:, :c]
        o = ws[:, c:] + _bmm(pick(p), u)
        s = pick(gl) * s + _bmm_tn(pick(kd), u)
        for h in range(nh):
            lanes = slice(h * DN_DK, (h + 1) * DN_DK)
            of_ref[ci * c:(ci + 1) * c, lanes] = o[h]
            ob_ref[cb * c:(cb + 1) * c, lanes] = o[nh + h]
    s = s.reshape(2, nh, DN_DK, DN_DK)
    s_ref[...] = s

    @pl.when(is_ctx)
    def _():
        sout_ref[...] = s


def _delta_rule(lay, qkv, gates, s0, layer_j):
    n_blocks = lay.n_tok // SEQ_BLOCK
    n_ctx_blocks = lay.n_ctx_tok // SEQ_BLOCK
    per_seq = lay.dec_len // SEQ_BLOCK
    vd = DN_HEADS * DN_DK

    def rev(i):
        j = i - n_ctx_blocks
        return jnp.where(i < n_ctx_blocks, i, n_ctx_blocks + (j // per_seq) * per_seq + per_seq - 1 - j % per_seq)

    fwd = lambda i: i
    cols = lambda blk, part: pl.BlockSpec((SEQ_BLOCK, vd), lambda i: (blk(i), part))
    gate_spec = lambda blk: pl.BlockSpec((SEQ_BLOCK, LANES), lambda i: (blk(i), 0))
    state_shape = (2, DN_HEADS, DN_DK, DN_DK)
    s0_spec = pl.BlockSpec((None, None) + state_shape,
                           lambda i: (jnp.maximum(i - n_ctx_blocks, 0) // per_seq, layer_j, 0, 0, 0, 0))
    sout_spec = pl.BlockSpec((None,) + state_shape,
                             lambda i: (jnp.minimum(i, n_ctx_blocks - 1), 0, 0, 0, 0))
    o_shape = jax.ShapeDtypeStruct((lay.n_tok, vd), F32)
    return pl.pallas_call(
        functools.partial(_delta_kernel, lay=lay),
        out_shape=(o_shape, o_shape,
                   jax.ShapeDtypeStruct((n_ctx_blocks,) + state_shape, F32)),
        grid=(n_blocks,),
        in_specs=[cols(fwd, 0), cols(fwd, 1), cols(fwd, 2), gate_spec(fwd),
                  cols(rev, 0), cols(rev, 1), cols(rev, 2), gate_spec(rev), s0_spec],
        out_specs=(cols(fwd, 0), cols(rev, 0), sout_spec),
        scratch_shapes=[pltpu.VMEM(state_shape, F32)],
        compiler_params=_cparams(1),
        name="delta_rule",
    )(qkv, qkv, qkv, gates, qkv, qkv, qkv, gates, s0)


def _dn_out_kernel(x_ref, of_ref, ob_ref, z_ref, mod_ref, ng_ref, w_ref, g_ref, b_ref, o_ref, act_ref,
                   *, alpha):
    for h in range(DN_HEADS):
        lanes = slice(h * DN_DK, (h + 1) * DN_DK)
        o = of_ref[:, lanes] + ob_ref[:, lanes]
        rms = lax.rsqrt(jnp.mean(o * o, axis=-1, keepdims=True) + LN_EPS)
        act_ref[:, lanes] = (o * rms * ng_ref[...] * _silu(z_ref[:, lanes])).astype(BF16)
    y = jnp.dot(act_ref[...], w_ref[...], preferred_element_type=F32)
    o_ref[...] = _residual_norm(x_ref[...], y, 1.0, mod_ref, 1, g_ref, b_ref, alpha)


def _dn_out(lay, x, o_f, o_b, z, mod, layer, norm_g, w_out, g, b, alpha):
    d = lay.d
    vd = DN_HEADS * DN_DK
    return pl.pallas_call(
        functools.partial(_dn_out_kernel, alpha=alpha),
        out_shape=jax.ShapeDtypeStruct((lay.n_tok, d), F32),
        grid=(lay.n_tok // TOK_TILE,),
        in_specs=[_row_spec(TOK_TILE, d), _row_spec(TOK_TILE, vd), _row_spec(TOK_TILE, vd),
                  _row_spec(TOK_TILE, vd), lay.mod_spec(layer, TOK_TILE),
                  _const_spec((1, DN_DK)), _const_spec((vd, d)), _const_spec((1, d)), _const_spec((1, d))],
        out_specs=_row_spec(TOK_TILE, d),
        scratch_shapes=[pltpu.VMEM((TOK_TILE, vd), BF16)],
        compiler_params=_cparams(1),
        name="dn_out",
    )(x, o_f, o_b, z, mod, norm_g.reshape(1, DN_DK), w_out, g.reshape(1, d), b.reshape(1, d))


def _dn_mixer(lay, x, mod, layer, w_in, conv_w, a_log, dt_bias, norm_g, w_out, s0, layer_j, g, b, alpha):
    qk, vd = DN_HEADS * DN_DK, DN_HEADS * DN_DK
    n_qkv = 2 * qk + vd
    n_gate = 4 * DN_HEADS
    assert w_in.shape[1] == n_qkv + vd + n_gate and n_gate <= LANES
    w_qkvz = w_in[:, :n_qkv + vd].astype(BF16)
    w_ab = jnp.pad(w_in[:, n_qkv + vd:], ((0, 0), (0, LANES - n_gate))).astype(BF16)
    lane_row = lambda v: jnp.pad(v.reshape(1, 2 * DN_HEADS), ((0, 0), (0, LANES - 2 * DN_HEADS)))
    assert conv_w.shape[0] == 3
    conv_pad = jnp.pad(conv_w, ((0, SUBLANES - conv_w.shape[0]), (0, 0)))
    qkv, z, gates = _dn_in(lay, x, mod, layer, w_qkvz, w_ab, conv_pad, lane_row(a_log), lane_row(dt_bias), n_qkv)
    o_f, o_b, s_fin = _delta_rule(lay, qkv, gates, s0, layer_j)
    x = _dn_out(lay, x, o_f, o_b, z, mod, layer, norm_g, w_out, g, b, alpha)
    return x, s_fin


def kernel(x_prompt, x_sample, state_delta, c, c_ctx, w_mod, b_mod, ln_g, ln_b, ffn_w_in, ffn_w_out,
           cv_w1, cv_b1, cv_dw, cv_dwb, cv_ln_g, cv_ln_b, cv_w2, cv_b2,
           dn_w_in, dn_conv, dn_a_log, dn_dt_bias, dn_norm_g, dn_w_out):
    n_ctx_seq, ctx_len, d = x_prompt.shape
    n_dec_seq, dec_len, _ = x_sample.shape
    depth = w_mod.shape[0]
    alpha = (2.0 * depth) ** 0.25
    lay = _Layout(n_ctx_seq, ctx_len, n_dec_seq, dec_len, d)

    x = (x_prompt.reshape(lay.n_ctx_tok, d), x_sample.reshape(lay.n_dec_tok, d))
    cond =jnp.concatenate([c_ctx[None, :], c, jnp.zeros((COND_ROWS - 1 - n_dec_seq, d), F32)], axis=0)
    mod = _adaln(cond, w_mod, b_mod).reshape(depth, COND_ROWS, N_MOD, d)

    states = []
    for l in range(depth):
        x = _ffn(lay, x, mod, l, 0, ffn_w_in[l, 0].astype(BF16), ffn_w_out[l, 0].astype(BF16),
                 ln_g[l, 0], ln_b[l, 0], alpha)
        j = l // N_MIXERS
        if l % N_MIXERS == 0:
            x = _conv_mixer(lay, x, mod, l, cv_w1[j].astype(BF16), cv_b1[j], cv_dw[j], cv_dwb[j],
                            cv_ln_g[j], cv_ln_b[j], cv_w2[j].astype(BF16), cv_b2[j],
                            ln_g[l, 1], ln_b[l, 1], alpha)
        else:
            x, s_fin = _dn_mixer(lay, x, mod, l, dn_w_in[j], dn_conv[j], dn_a_log[j], dn_dt_bias[j],
                                 dn_norm_g[j], dn_w_out[j].astype(BF16), state_delta, j,
                                 ln_g[l, 1], ln_b[l, 1], alpha)
            states.append(s_fin)
        x = _ffn(lay, x, mod, l, 2, ffn_w_in[l, 1].astype(BF16), ffn_w_out[l, 1].astype(BF16),
                 ln_g[l, 2], ln_b[l, 2], alpha)

    y_prompt = x[:lay.n_ctx_tok].reshape(n_ctx_seq, ctx_len, d)
    y_sample = x[lay.n_ctx_tok:].reshape(n_dec_seq, dec_len, d)
    new_state = jnp.stack(states, axis=1).astype(state_delta.dtype)
    return y_prompt, y_sample, new_state
```

```python
import functools

import jax
import jax.numpy as jnp
from jax import lax
from jax.experimental import pallas as pl
from jax.experimental.pallas import tpu as pltpu

F32 = jnp.float32
BF16 = jnp.bfloat16

LN_EPS = 1e-5
L2_EPS = 1e-6
N_MOD = 9
N_MIXERS = 2
GRID_W = 64
DN_HEADS = 8
DN_DK = 128
DN_CHUNK = 64
COND_ROWS = 8
LANES = 128
VMEM_LIMIT = 56 * 1024 * 1024

TOK_TILE = 512
SEQ_BLOCK = 256
FF_CHUNK = 256


def _cparams(n_axes):
    return pltpu.CompilerParams(dimension_semantics=("arbitrary",) * n_axes,
                                vmem_limit_bytes=VMEM_LIMIT)


def _mm(a, b):
    return jnp.dot(a.astype(BF16), b.astype(BF16), preferred_element_type=F32)


def _mm_nt(a, b):
    return lax.dot_general(a.astype(BF16), b.astype(BF16), (((1,), (1,)), ((), ())),
                           preferred_element_type=F32)


def _mm_tn(a, b):
    return lax.dot_general(a.astype(BF16), b.astype(BF16), (((0,), (0,)), ((), ())),
                           preferred_element_type=F32)


def _sigmoid(x):
    return 1.0 / (1.0 + jnp.exp(-x))


def _silu(x):
    return x * _sigmoid(x)


def _layer_norm(r, g, b):
    mu = jnp.mean(r, axis=-1, keepdims=True)
    rc = r - mu
    var = jnp.mean(rc * rc, axis=-1, keepdims=True)
    return rc * lax.rsqrt(var + LN_EPS) * g + b


def _modulate(x, mod_ref, slot):
    shift = mod_ref[3 * slot:3 * slot + 1, :]
    scale = mod_ref[3 * slot + 1:3 * slot + 2, :]
    return x * (1.0 + scale) + shift


def _residual_norm(x, y, gate_scale, mod_ref, slot, g_ref, b_ref, alpha):
    gate = mod_ref[3 * slot + 2:3 * slot + 3, :]
    return _layer_norm(alpha * x + (gate_scale * gate) * y, g_ref[...], b_ref[...])


class _Layout:
    def __init__(self, n_ctx_seq, ctx_len, n_dec_seq, dec_len, d_model):
        self.n_ctx_seq, self.ctx_len = n_ctx_seq, ctx_len
        self.n_dec_seq, self.dec_len = n_dec_seq, dec_len
        self.n_ctx_tok = n_ctx_seq * ctx_len
        self.n_dec_tok = n_dec_seq * dec_len
        self.n_tok = self.n_ctx_tok + self.n_dec_tok
        self.d = d_model
        assert n_dec_seq + 1 <= COND_ROWS
        assert self.n_ctx_tok % TOK_TILE == 0 and dec_len % TOK_TILE == 0
        assert ctx_len == SEQ_BLOCK and dec_len % SEQ_BLOCK == 0
        assert dec_len % GRID_W == 0 and SEQ_BLOCK % GRID_W == 0

    def mod_row(self, i, tile):
        n_ctx_tiles = self.n_ctx_tok // tile
        per_seq = self.dec_len // tile
        return jnp.where(i < n_ctx_tiles, 0, 1 + (i - n_ctx_tiles) // per_seq)

    def mod_spec(self, layer, tile):
        return pl.BlockSpec((None, None, N_MOD, self.d),
                            lambda i: (layer, self.mod_row(i, tile), 0, 0))

    def group_specs(self, tile, width):
        n_ctx_tiles = self.n_ctx_tok // tile
        return (pl.BlockSpec((tile, width), lambda i: (jnp.minimum(i, n_ctx_tiles - 1), 0)),
                pl.BlockSpec((tile, width), lambda i: (jnp.maximum(i - n_ctx_tiles, 0), 0)))


def _row_spec(tile, width, col=0):
    return pl.BlockSpec((tile, width), lambda i: (i, col))


def _const_spec(shape):
    return pl.BlockSpec(shape, lambda *_: (0,) * len(shape))


def _stacked_spec(stacked, index):
    shape = stacked.shape[1:]
    return pl.BlockSpec((None,) + shape, lambda *_: (index,) + (0,) * len(shape))


def _adaln_kernel(cond_ref, w_ref, b_ref, o_ref):
    cnd = cond_ref[...]
    o_ref[...] = jnp.dot(_silu(cnd), w_ref[...], preferred_element_type=F32,
                         precision=lax.Precision.HIGHEST) + b_ref[...]


def _adaln(cond, w_mod, b_mod):
    depth, d, n_out = w_mod.shape
    bn = n_out // 4
    return pl.pallas_call(
        _adaln_kernel,
        out_shape=jax.ShapeDtypeStruct((depth, COND_ROWS, n_out), F32),
        grid=(depth, n_out // bn),
        in_specs=[pl.BlockSpec((COND_ROWS, d), lambda l, j: (0, 0)),
                  pl.BlockSpec((None, d, bn), lambda l, j: (l, 0, j)),
                  pl.BlockSpec((None, 1, bn), lambda l, j: (l, 0, j))],
        out_specs=pl.BlockSpec((None, COND_ROWS, bn), lambda l, j: (l, 0, j)),
        compiler_params=_cparams(2),
        name="adaln",
    )(cond, w_mod, b_mod.reshape(depth, 1, n_out))


def _ffn_kernel(*refs, slot, d_ff, alpha, n_ctx_tiles, split_in, split_out):
    n_in = 2 if split_in else 1
    n_out = 2 if split_out else 1
    x_refs, (mod_ref, w_in_ref, w_out_ref, g_ref, b_ref) = refs[:n_in], refs[n_in:n_in + 5]
    o_refs, act_ref = refs[n_in + 5:n_in + 5 + n_out], refs[-1]
    is_ctx = pl.program_id(0) < n_ctx_tiles
    x = jnp.where(is_ctx, x_refs[0][...], x_refs[1][...]) if split_in else x_refs[0][...]
    h = _modulate(x, mod_ref, slot).astype(BF16)
    for j in range(d_ff // FF_CHUNK):
        lo = j * FF_CHUNK
        lin = jnp.dot(h, w_in_ref[:, lo:lo + FF_CHUNK], preferred_element_type=F32)
        gat = jnp.dot(h, w_in_ref[:, d_ff + lo:d_ff + lo + FF_CHUNK], preferred_element_type=F32)
        act_ref[:, lo:lo + FF_CHUNK] = (_silu(gat) * lin).astype(BF16)
    y = jnp.dot(act_ref[...], w_out_ref[...], preferred_element_type=F32)
    out = _residual_norm(x, y, 0.5, mod_ref, slot, g_ref, b_ref, alpha)
    if split_out:
        @pl.when(is_ctx)
        def _():
            o_refs[0][...] = out

        @pl.when(jnp.logical_not(is_ctx))
        def _():
            o_refs[1][...] = out
    else:
        o_refs[0][...] = out


def _ffn(lay, xs, mod, layer, sub, w_in, w_out, g, b, alpha, split_out=False):
    d, d_ff = lay.d, w_out.shape[2]
    assert d_ff % FF_CHUNK == 0
    split_in = isinstance(xs, tuple)
    x_specs = list(lay.group_specs(TOK_TILE, d)) if split_in else [_row_spec(TOK_TILE, d)]
    if split_out:
        out_shape = (jax.ShapeDtypeStruct((lay.n_ctx_tok, d), F32), jax.ShapeDtypeStruct((lay.n_dec_tok, d), F32))
        out_specs = lay.group_specs(TOK_TILE, d)
    else:
        out_shape = jax.ShapeDtypeStruct((lay.n_tok, d), F32)
        out_specs = _row_spec(TOK_TILE, d)
    return pl.pallas_call(
        functools.partial(_ffn_kernel, slot=2 * sub, d_ff=d_ff, alpha=alpha,
                          n_ctx_tiles=lay.n_ctx_tok // TOK_TILE, split_in=split_in, split_out=split_out),
        out_shape=out_shape,
        grid=(lay.n_tok // TOK_TILE,),
        in_specs=x_specs + [lay.mod_spec(layer, TOK_TILE),
                            pl.BlockSpec((None, None, d, 2 * d_ff), lambda i: (layer, sub, 0, 0)),
                            pl.BlockSpec((None, None, d_ff, d), lambda i: (layer, sub, 0, 0)),
                            pl.BlockSpec((None, None, 1, d), lambda i: (layer, 2 * sub, 0, 0)),
                            pl.BlockSpec((None, None, 1, d), lambda i: (layer, 2 * sub, 0, 0))],
        out_specs=out_specs,
        scratch_shapes=[pltpu.VMEM((TOK_TILE, d_ff), BF16)],
        compiler_params=_cparams(1),
        name="ffn",
    )(*(xs if split_in else (xs,)), mod, w_in, w_out, g, b)


def _conv_in_kernel(x_ref, mod_ref, w1_ref, b1_ref, u_ref, *, d):
    h = _modulate(x_ref[...], mod_ref, 1).astype(BF16)
    lin = jnp.dot(h, w1_ref[:, :d], preferred_element_type=F32) + b1_ref[:, :d]
    gat = jnp.dot(h, w1_ref[:, d:], preferred_element_type=F32) + b1_ref[:, d:]
    u_ref[...] = lin * _sigmoid(gat)


def _conv_in(lay, x, mod, layer, w1, b1):
    d = lay.d
    return pl.pallas_call(
        functools.partial(_conv_in_kernel, d=d),
        out_shape=jax.ShapeDtypeStruct((lay.n_tok, d), F32),
        grid=(lay.n_tok // TOK_TILE,),
        in_specs=[_row_spec(TOK_TILE, d), lay.mod_spec(layer, TOK_TILE),
                  _stacked_spec(w1, layer // N_MIXERS), _const_spec((1, 2 * d))],
        out_specs=_row_spec(TOK_TILE, d),
        compiler_params=_cparams(1),
        name="conv_in",
    )(x, mod, w1, b1.reshape(1, 2 * d))


CONV_PAD = 16
CONV_ROWS = 64


SUBLANES = 8
DEC_CONV_CH = 256


def _segment_conv(u_ref, w_ref, o_ref, pad_ref, sh_ref, *, taps, seg_len, n_seg):
    n_ch = u_ref.shape[1]
    half = taps // 2
    n_sh = sh_ref.shape[1]
    zeros = jnp.zeros((CONV_PAD, n_ch), F32)
    pad_ref[0:CONV_PAD, :] = zeros
    pad_ref[CONV_PAD + seg_len:2 * CONV_PAD + seg_len, :] = zeros

    def segment(s, carry):
        row0 = pl.multiple_of(s * seg_len, seg_len)
        pad_ref[CONV_PAD:CONV_PAD + seg_len, :] = u_ref[pl.ds(row0, seg_len), :]

        def lane_tile(j, carry):
            lanes = pl.ds(pl.multiple_of(j * LANES, LANES), LANES)
            for p in range(1, SUBLANES):
                sh_ref[p] = pad_ref[p:p + n_sh, lanes]
            for r in range(seg_len // CONV_ROWS):
                acc = jnp.zeros((CONV_ROWS, LANES), F32)
                for k in range(taps):
                    off = CONV_PAD - half + k
                    p = off % SUBLANES
                    start = off - p + r * CONV_ROWS
                    src = (pad_ref[start:start + CONV_ROWS, lanes] if p == 0
                           else sh_ref[p, start:start + CONV_ROWS, :])
                    acc = acc + w_ref[k:k + 1, lanes] * src
                o_ref[pl.ds(row0 + r * CONV_ROWS, CONV_ROWS), lanes] = acc
            return carry

        return lax.fori_loop(0, n_ch // LANES, lane_tile, carry)

    lax.fori_loop(0, n_seg, segment, 0)


def _segment_conv_scratch(seg_len, n_ch):
    return [pltpu.VMEM((seg_len + 2 * CONV_PAD, n_ch), F32),
            pltpu.VMEM((SUBLANES, seg_len + 2 * CONV_PAD - SUBLANES, LANES), F32)]


def _ctx_conv_kernel(u_ref, w_ref, o_ref, pad_ref, sh_ref, *, taps):
    _segment_conv(u_ref, w_ref, o_ref, pad_ref, sh_ref, taps=taps, seg_len=u_ref.shape[0], n_seg=1)


def _ctx_conv(lay, u, w, taps):
    d = lay.d
    return pl.pallas_call(
        functools.partial(_ctx_conv_kernel, taps=taps),
        out_shape=jax.ShapeDtypeStruct((lay.n_ctx_tok, d), F32),
        grid=(lay.n_ctx_seq,),
        in_specs=[pl.BlockSpec((lay.ctx_len, d), lambda i: (i, 0)),
                  pl.BlockSpec((w.shape[0], d), lambda i: (0, 0))],
        out_specs=pl.BlockSpec((lay.ctx_len, d), lambda i: (i, 0)),
        scratch_shapes=_segment_conv_scratch(lay.ctx_len, d),
        compiler_params=_cparams(1),
        name="ctx_conv",
    )(u, w)


def _dec_conv_kernel(u_ref, w_ref, o_ref, pad_ref, sh_ref, vpad_ref, *, taps, n_row_blocks):
    j = pl.program_id(1)
    seq_len, n_ch = u_ref.shape

    @pl.when(j < n_row_blocks)
    def _():
        _segment_conv(u_ref, w_ref, o_ref, pad_ref, sh_ref, taps=taps, seg_len=GRID_W,
                      n_seg=seq_len // GRID_W)

    @pl.when(j >= n_row_blocks)
    def _():
        halo = (taps // 2) * GRID_W
        zeros = jnp.zeros((halo, n_ch), F32)
        vpad_ref[0:halo, :] = zeros
        vpad_ref[halo + seq_len:2 * halo + seq_len, :] = zeros
        vpad_ref[halo:halo + seq_len, :] = u_ref[...]

        def row_chunk(r, carry):
            base = pl.multiple_of(r * CONV_ROWS, CONV_ROWS)
            for t in range(n_ch // LANES):
                lanes = slice(t * LANES, (t + 1) * LANES)
                acc = jnp.zeros((CONV_ROWS, LANES), F32)
                for k in range(taps):
                    acc = acc + w_ref[k:k + 1, lanes] * vpad_ref[pl.ds(base + k * GRID_W, CONV_ROWS), lanes]
                o_ref[pl.ds(base, CONV_ROWS), lanes] = acc
            return carry

        lax.fori_loop(0, seq_len // CONV_ROWS, row_chunk, 0)


def _dec_conv(lay, u, w, taps):
    d = lay.d
    first_seq = lay.n_ctx_tok // lay.dec_len
    assert lay.n_ctx_tok % lay.dec_len == 0 and (d // 2) % DEC_CONV_CH == 0
    halo = (taps // 2) * GRID_W
    return pl.pallas_call(
        functools.partial(_dec_conv_kernel, taps=taps, n_row_blocks=(d // 2) // DEC_CONV_CH),
        out_shape=jax.ShapeDtypeStruct((lay.n_dec_tok, d), F32),
        grid=(lay.n_dec_seq, d // DEC_CONV_CH),
        in_specs=[pl.BlockSpec((lay.dec_len, DEC_CONV_CH), lambda b, j: (first_seq + b, j)),
                  pl.BlockSpec((w.shape[0], DEC_CONV_CH), lambda b, j: (0, j))],
        out_specs=pl.BlockSpec((lay.dec_len, DEC_CONV_CH), lambda b, j: (b, j)),
        scratch_shapes=_segment_conv_scratch(GRID_W, DEC_CONV_CH)
        + [pltpu.VMEM((lay.dec_len + 2 * halo, DEC_CONV_CH), F32)],
        compiler_params=_cparams(2),
        name="dec_conv",
    )(u, w)


def _conv_out_kernel(x_ref, cvc_ref, cvd_ref, mod_ref, dwb_ref, lng_ref, lnb_ref, w2_ref, b2_ref, g_ref, b_ref,
                     o_ref, *, alpha, n_ctx_tiles):
    cv = jnp.where(pl.program_id(0) < n_ctx_tiles, cvc_ref[...], cvd_ref[...])
    a = _silu(_layer_norm(cv + dwb_ref[...], lng_ref[...], lnb_ref[...]))
    y = jnp.dot(a.astype(BF16), w2_ref[...], preferred_element_type=F32) + b2_ref[...]
    o_ref[...] = _residual_norm(x_ref[...], y, 1.0, mod_ref, 1, g_ref, b_ref, alpha)


def _conv_out(lay, x, cv_ctx, cv_dec, mod, layer, dwb, lng, lnb, w2, b2, g, b, alpha):
    d = lay.d
    vec = lambda v: v.reshape(1, d)
    return pl.pallas_call(
        functools.partial(_conv_out_kernel, alpha=alpha, n_ctx_tiles=lay.n_ctx_tok // TOK_TILE),
        out_shape=jax.ShapeDtypeStruct((lay.n_tok, d), F32),
        grid=(lay.n_tok // TOK_TILE,),
        in_specs=[_row_spec(TOK_TILE, d), *lay.group_specs(TOK_TILE, d), lay.mod_spec(layer, TOK_TILE),
                  _const_spec((1, d)), _const_spec((1, d)), _const_spec((1, d)),
                  _stacked_spec(w2, layer // N_MIXERS), _const_spec((1, d)), _const_spec((1, d)),
                  _const_spec((1, d))],
        out_specs=_row_spec(TOK_TILE, d),
        compiler_params=_cparams(1),
        name="conv_out",
    )(x, cv_ctx, cv_dec, mod, vec(dwb), vec(lng), vec(lnb), w2, vec(b2), vec(g), vec(b))


def _conv_mixer(lay, x, mod, layer, w1, b1, dw, dwb, lng, lnb, w2, b2, g, b, alpha):
    taps = dw.shape[0]
    assert taps // 2 < CONV_PAD
    u = _conv_in(lay, x, mod, layer, w1, b1)
    dw_pad = jnp.pad(dw, ((0, -taps % SUBLANES), (0, 0)))
    cv_ctx = _ctx_conv(lay, u, dw_pad, taps)
    cv_dec = _dec_conv(lay, u, dw_pad, taps)
    return _conv_out(lay, x, cv_ctx, cv_dec, mod, layer, dwb, lng, lnb, w2, b2, g, b, alpha)


def _split3(x):
    hi = x.astype(BF16)
    r1 = x - hi.astype(F32)
    mid = r1.astype(BF16)
    lo = (r1 - mid.astype(F32)).astype(BF16)
    return hi, mid, lo


def _chunk_cumsum(tri, x):
    hi, mid, lo = _split3(x)
    dot = lambda t: jnp.dot(tri, t, preferred_element_type=F32)
    return dot(hi) + dot(mid) + dot(lo)


DN_HALO = 16
DN_COLS = 512


def _dn_in_kernel(x_ref, xp_ref, xn_ref, mod_ref, w_ref, wab_ref, cw_ref, alog_ref, dtb_ref,
                  qkv_ref, z_ref, gates_ref, h_ref, *, lay, n_qkv):
    i = pl.program_id(0)
    n_ctx_blocks = lay.n_ctx_tok // SEQ_BLOCK
    per_seq = lay.dec_len // SEQ_BLOCK
    pos = (i - n_ctx_blocks) % per_seq
    is_ctx = i < n_ctx_blocks
    has_prev = jnp.logical_not(is_ctx | (pos == 0))
    has_next = jnp.logical_not(is_ctx | (pos == per_seq - 1))
    lo, hi = DN_HALO, DN_HALO + SEQ_BLOCK
    h_ref[0:lo] = jnp.where(has_prev, _modulate(xp_ref[...], mod_ref, 1), 0.0).astype(BF16)
    h_ref[lo:hi] = _modulate(x_ref[...], mod_ref, 1).astype(BF16)
    h_ref[hi:hi + DN_HALO] = jnp.where(has_next, _modulate(xn_ref[...], mod_ref, 1), 0.0).astype(BF16)

    n_qk = 2 * DN_HEADS * DN_DK
    for j in range(n_qkv // DN_COLS):
        cols = slice(j * DN_COLS, (j + 1) * DN_COLS)
        pre = jnp.dot(h_ref[...], w_ref[:, cols], preferred_element_type=F32)
        n_rows = pre.shape[0]
        y = _silu(cw_ref[0:1, cols] * pltpu.roll(pre, 1, 0)[lo:hi] + cw_ref[1:2, cols] * pre[lo:hi]
                  + cw_ref[2:3, cols] * pltpu.roll(pre, n_rows - 1, 0)[lo:hi])
        for t in range(DN_COLS // DN_DK):
            c0 = j * DN_COLS + t * DN_DK
            yt = y[:, t * DN_DK:(t + 1) * DN_DK]
            if c0 < n_qk:
                inv = lax.rsqrt(jnp.sum(yt * yt, axis=-1, keepdims=True) + L2_EPS)
                yt = yt * (inv * (DN_DK ** -0.5 if c0 < n_qk // 2 else 1.0))
            qkv_ref[:, c0:c0 + DN_DK] = yt

    h = h_ref[lo:hi]
    z_ref[...] = jnp.dot(h, w_ref[:, n_qkv:], preferred_element_type=F32)
    ab = jnp.dot(h, wab_ref[...], preferred_element_type=F32)
    pre = ab + dtb_ref[...]
    softplus = jnp.maximum(pre, 0.0) + jnp.log(1.0 + jnp.exp(-jnp.abs(pre)))
    log_a = -jnp.exp(alog_ref[...]) * softplus
    beta = _sigmoid(ab)
    tile = ab.shape[0]
    row = lax.broadcasted_iota(jnp.int32, (tile, tile), 0)
    col = lax.broadcasted_iota(jnp.int32, (tile, tile), 1)
    same = (row // DN_CHUNK) == (col // DN_CHUNK)
    lower = (same & (row >= col)).astype(BF16)
    upper = (same & (row <= col)).astype(BF16)
    lane = lax.broadcasted_iota(jnp.int32, ab.shape, 1)
    g_fwd = _chunk_cumsum(lower, log_a)
    g_bwd = _chunk_cumsum(upper, log_a)
    gates_ref[...] = jnp.where(lane < DN_HEADS, g_fwd, jnp.where(lane < 2 * DN_HEADS, g_bwd, beta))


def _dn_in(lay, x, mod, layer, w_qkvz, w_ab, conv_w, alog_row, dtb_row, n_qkv):
    d, n_proj = w_qkvz.shape
    assert n_qkv % DN_COLS == 0 and SEQ_BLOCK % DN_HALO == 0
    sub = SEQ_BLOCK // DN_HALO
    n_units = lay.n_tok // DN_HALO
    return pl.pallas_call(
        functools.partial(_dn_in_kernel, lay=lay, n_qkv=n_qkv),
        out_shape=(jax.ShapeDtypeStruct((lay.n_tok, n_qkv), F32),
                   jax.ShapeDtypeStruct((lay.n_tok, n_proj - n_qkv), F32),
                   jax.ShapeDtypeStruct((lay.n_tok, LANES), F32)),
        grid=(lay.n_tok // SEQ_BLOCK,),
        in_specs=[_row_spec(SEQ_BLOCK, d),
                  pl.BlockSpec((DN_HALO, d), lambda i: (jnp.maximum(i * sub - 1, 0), 0)),
                  pl.BlockSpec((DN_HALO, d), lambda i: (jnp.minimum((i + 1) * sub, n_units - 1), 0)),
                  lay.mod_spec(layer, SEQ_BLOCK),
                  _const_spec((d, n_proj)), _const_spec((d, LANES)), _const_spec((SUBLANES, n_qkv)),
                  _const_spec((1, LANES)), _const_spec((1, LANES))],
        out_specs=(_row_spec(SEQ_BLOCK, n_qkv), _row_spec(SEQ_BLOCK, n_proj - n_qkv),
                   _row_spec(SEQ_BLOCK, LANES)),
        scratch_shapes=[pltpu.VMEM((SEQ_BLOCK + 2 * DN_HALO, d), BF16)],
        compiler_params=_cparams(1),
        name="dn_in",
    )(x, x, x, mod, w_qkvz, w_ab, conv_w, alog_row, dtb_row)


def _bmm(a, b):
    return jnp.einsum("bmk,bkn->bmn", a.astype(BF16), b.astype(BF16), preferred_element_type=F32)


def _bmm_nt(a, b):
    return jnp.einsum("bmk,bnk->bmn", a.astype(BF16), b.astype(BF16), preferred_element_type=F32)


def _bmm_tn(a, b):
    return jnp.einsum("bkm,bkn->bmn", a.astype(BF16), b.astype(BF16), preferred_element_type=F32)


def _chunk_masks():
    c = DN_CHUNK
    row = lax.broadcasted_iota(jnp.int32, (c, c), 0)
    col = lax.broadcasted_iota(jnp.int32, (c, c), 1)
    same = lambda n: (row // n) == (col // n)
    levels = []
    n = 8
    while n < c:
        levels.append(same(2 * n) & jnp.logical_not(same(n)))
        n *= 2
    return row, col, same(8), levels


def _dir_where(mask_fwd, mask_bwd, x, other):
    half = x.shape[0] // 2
    return jnp.concatenate([jnp.where(mask_fwd[None], x[:half], other),
                            jnp.where(mask_bwd[None], x[half:], other)], axis=0)


def _unit_tri_inverse_minus_eye(a, base_mask, level_masks):
    d = jnp.where(base_mask[None], a, 0.0)
    x = _bmm(d, d)
    x2 = _bmm(x, x)
    m1 = x - d - _bmm(d, x)
    n = m1 + x2 + _bmm(m1, x2)
    for mask in level_masks:
        e = jnp.where(mask[None], a, 0.0)
        y = e + _bmm(n, e)
        n = n - y - _bmm(y, n)
    return n


def _lane_broadcast(x, lane):
    lanes = lax.broadcasted_iota(jnp.int32, x.shape, 1)
    col = jnp.sum(jnp.where(lanes == lane, x, 0.0), axis=-1, keepdims=True)
    return jnp.broadcast_to(col, x.shape)


def _delta_kernel(*refs, lay, n_prev):
    if n_prev:
        (qf_ref, kf_ref, vf_ref, gf_ref, qb_ref, kb_ref, vb_ref, gb_ref, s0_ref, prev_ref,
         of_ref, ob_ref, sout_ref, s_ref) = refs
    else:
        (qf_ref, kf_ref, vf_ref, gf_ref, qb_ref, kb_ref, vb_ref, gb_ref, s0_ref,
         of_ref, ob_ref, sout_ref, s_ref) = refs
    i = pl.program_id(0)
    n_ctx_blocks = lay.n_ctx_tok // SEQ_BLOCK
    per_seq = lay.dec_len // SEQ_BLOCK
    nc, nh, c = SEQ_BLOCK // DN_CHUNK, DN_HEADS, DN_CHUNK
    is_ctx = i < n_ctx_blocks
    seq_start = is_ctx | ((i - n_ctx_blocks) % per_seq == 0)

    @pl.when(seq_start)
    def _():
        s_ref[...] = jnp.where(is_ctx, 0.0, s0_ref[...])

    views = ((qf_ref, kf_ref, vf_ref, gf_ref), (qb_ref, kb_ref, vb_ref, gb_ref))
    qs, ks, vs, gs, bs, grs = [], [], [], [], [], []
    for d, (q_ref, k_ref, v_ref, g_ref) in enumerate(views):
        gates_all = g_ref[...]
        gates_t = jnp.transpose(gates_all)
        for ci in range(nc):
            rows = slice(ci * c, (ci + 1) * c)
            gates = gates_all[rows]
            for h in range(nh):
                lanes = slice(h * DN_DK, (h + 1) * DN_DK)
                lane = d * nh + h
                qs.append(q_ref[rows, lanes])
                ks.append(k_ref[rows, lanes])
                vs.append(v_ref[rows, lanes])
                gs.append(_lane_broadcast(gates, lane))
                bs.append(_lane_broadcast(gates, 2 * nh + lane))
                grs.append(gates_t[lane:lane + 1, rows])
    q, k, v = jnp.stack(qs), jnp.stack(ks), jnp.stack(vs)
    g, beta, g_row = jnp.stack(gs), jnp.stack(bs), jnp.stack(grs)
    half = nc * nh

    row, col, base_mask, level_masks = _chunk_masks()
    qk_kk = _bmm_nt(jnp.concatenate([q, k], axis=1), k)
    decay = jnp.exp(_dir_where(row >= col, row <= col, g[:, :, :c] - g_row, -jnp.inf))
    a = beta[:, :, :c] * qk_kk[:, c:] * _dir_where(row > col, row < col, decay, 0.0)
    p = qk_kk[:, :c] * decay
    n = _unit_tri_inverse_minus_eye(a, base_mask, level_masks)
    eg = jnp.exp(g)
    g_last = jnp.concatenate([g[:half, c - 1:c], g[half:, 0:1]], axis=0)
    rhs = jnp.concatenate([beta * v, (beta * eg) * k], axis=2)
    sol = rhs + _bmm(n, rhs)
    u_t = sol[:, :, :DN_DK]
    wq = jnp.concatenate([sol[:, :, DN_DK:], q * eg], axis=1)
    kd = k * jnp.exp(g_last - g)
    gl = jnp.exp(g_last)

    s = s_ref[...].reshape(2 * nh, DN_DK, DN_DK)
    for ci in range(nc):
        cb = nc - 1 - ci
        pick = lambda x: jnp.concatenate([x[ci * nh:(ci + 1) * nh],
                                          x[half + cb * nh:half + (cb + 1) * nh]], axis=0)
        ws = _bmm(pick(wq), s)
        u = pick(u_t) - ws[:, :c]
        o = ws[:, c:] + _bmm(pick(p), u)
        s = pick(gl) * s + _bmm_tn(pick(kd), u)
        for h in range(nh):
            lanes = slice(h * DN_DK, (h + 1) * DN_DK)
            of_ref[ci * c:(ci + 1) * c, lanes] = o[h]
            ob_ref[cb * c:(cb + 1) * c, lanes] = o[nh + h]
    s = s.reshape(2, nh, DN_DK, DN_DK)
    s_ref[...] = s

    @pl.when(is_ctx)
    def _():
        if n_prev:
            sout_ref[0:n_prev] = prev_ref[...]
        sout_ref[n_prev] = s


def _delta_rule(lay, qkv, gates, s0, layer_j, prev_states):
    n_prev = 0 if prev_states is None else prev_states.shape[1]
    assert n_prev == layer_j
    n_blocks = lay.n_tok // SEQ_BLOCK
    n_ctx_blocks = lay.n_ctx_tok // SEQ_BLOCK
    per_seq = lay.dec_len // SEQ_BLOCK
    vd = DN_HEADS * DN_DK

    def rev(i):
        j = i - n_ctx_blocks
        return jnp.where(i < n_ctx_blocks, i, n_ctx_blocks + (j // per_seq) * per_seq + per_seq - 1 - j % per_seq)

    fwd = lambda i: i
    cols = lambda blk, part: pl.BlockSpec((SEQ_BLOCK, vd), lambda i: (blk(i), part))
    gate_spec = lambda blk: pl.BlockSpec((SEQ_BLOCK, LANES), lambda i: (blk(i), 0))
    state_shape = (2, DN_HEADS, DN_DK, DN_DK)
    s0_spec = pl.BlockSpec((None, None) + state_shape,
                           lambda i: (jnp.maximum(i - n_ctx_blocks, 0) // per_seq, layer_j, 0, 0, 0, 0))
    ctx_states = lambda n: pl.BlockSpec((None, n) + state_shape,
                                        lambda i: (jnp.minimum(i, n_ctx_blocks - 1), 0, 0, 0, 0, 0))
    o_shape = jax.ShapeDtypeStruct((lay.n_tok, vd), F32)
    return pl.pallas_call(
        functools.partial(_delta_kernel, lay=lay, n_prev=n_prev),
        out_shape=(o_shape, o_shape,
                   jax.ShapeDtypeStruct((n_ctx_blocks, n_prev + 1) + state_shape, F32)),
        grid=(n_blocks,),
        in_specs=[cols(fwd, 0), cols(fwd, 1), cols(fwd, 2), gate_spec(fwd),
                  cols(rev, 0), cols(rev, 1), cols(rev, 2), gate_spec(rev), s0_spec]
        + ([ctx_states(n_prev)] if n_prev else []),
        out_specs=(cols(fwd, 0), cols(rev, 0), ctx_states(n_prev + 1)),
        scratch_shapes=[pltpu.VMEM(state_shape, F32)],
        compiler_params=_cparams(1),
        name="delta_rule",
    )(qkv, qkv, qkv, gates, qkv, qkv, qkv, gates, s0, *([prev_states] if n_prev else []))


def _dn_out_kernel(x_ref, of_ref, ob_ref, z_ref, mod_ref, ng_ref, w_ref, g_ref, b_ref, o_ref, act_ref,
                   *, alpha):
    for h in range(DN_HEADS):
        lanes = slice(h * DN_DK, (h + 1) * DN_DK)
        o = of_ref[:, lanes] + ob_ref[:, lanes]
        rms = lax.rsqrt(jnp.mean(o * o, axis=-1, keepdims=True) + LN_EPS)
        act_ref[:, lanes] = (o * rms * ng_ref[...] * _silu(z_ref[:, lanes])).astype(BF16)
    y = jnp.dot(act_ref[...], w_ref[...], preferred_element_type=F32)
    o_ref[...] = _residual_norm(x_ref[...], y, 1.0, mod_ref, 1, g_ref, b_ref, alpha)


def _dn_out(lay, x, o_f, o_b, z, mod, layer, norm_g, w_out, g, b, alpha):
    d = lay.d
    vd = DN_HEADS * DN_DK
    return pl.pallas_call(
        functools.partial(_dn_out_kernel, alpha=alpha),
        out_shape=jax.ShapeDtypeStruct((lay.n_tok, d), F32),
        grid=(lay.n_tok // TOK_TILE,),
        in_specs=[_row_spec(TOK_TILE, d), _row_spec(TOK_TILE, vd), _row_spec(TOK_TILE, vd),
                  _row_spec(TOK_TILE, vd), lay.mod_spec(layer, TOK_TILE),
                  _const_spec((1, DN_DK)), _stacked_spec(w_out, layer // N_MIXERS), _const_spec((1, d)),
                  _const_spec((1, d))],
        out_specs=_row_spec(TOK_TILE, d),
        scratch_shapes=[pltpu.VMEM((TOK_TILE, vd), BF16)],
        compiler_params=_cparams(1),
        name="dn_out",
    )(x, o_f, o_b, z, mod, norm_g.reshape(1, DN_DK), w_out, g.reshape(1, d), b.reshape(1, d))


def _dn_mixer(lay, x, mod, layer, w_in, conv_w, a_log, dt_bias, norm_g, w_out, s0, prev_states, g, b, alpha):
    layer_j = layer // N_MIXERS
    qk, vd = DN_HEADS * DN_DK, DN_HEADS * DN_DK
    n_qkv = 2 * qk + vd
    n_gate = 4 * DN_HEADS
    assert w_in.shape[1] == n_qkv + vd + n_gate and n_gate <= LANES
    w_qkvz = w_in[:, :n_qkv + vd].astype(BF16)
    w_ab = jnp.pad(w_in[:, n_qkv + vd:], ((0, 0), (0, LANES - n_gate))).astype(BF16)
    lane_row = lambda v: jnp.pad(v.reshape(1, 2 * DN_HEADS), ((0, 0), (0, LANES - 2 * DN_HEADS)))
    assert conv_w.shape[0] == 3
    conv_pad = jnp.pad(conv_w, ((0, SUBLANES - conv_w.shape[0]), (0, 0)))
    qkv, z, gates = _dn_in(lay, x, mod, layer, w_qkvz, w_ab, conv_pad, lane_row(a_log), lane_row(dt_bias), n_qkv)
    o_f, o_b, s_fin = _delta_rule(lay, qkv, gates, s0, layer_j, prev_states)
    x = _dn_out(lay, x, o_f, o_b, z, mod, layer, norm_g, w_out, g, b, alpha)
    return x, s_fin


def kernel(x_prompt, x_sample, state_delta, c, c_ctx, w_mod, b_mod, ln_g, ln_b, ffn_w_in, ffn_w_out,
           cv_w1, cv_b1, cv_dw, cv_dwb, cv_ln_g, cv_ln_b, cv_w2, cv_b2,
           dn_w_in, dn_conv, dn_a_log, dn_dt_bias, dn_norm_g, dn_w_out):
    n_ctx_seq, ctx_len, d = x_prompt.shape
    n_dec_seq, dec_len, _ = x_sample.shape
    depth = w_mod.shape[0]
    alpha = (2.0 * depth) ** 0.25
    lay = _Layout(n_ctx_seq, ctx_len, n_dec_seq, dec_len, d)

    x = (x_prompt.reshape(lay.n_ctx_tok, d), x_sample.reshape(lay.n_dec_tok, d))
    cond = jnp.concatenate([c_ctx[None, :], c, jnp.zeros((COND_ROWS - 1 - n_dec_seq, d), F32)], axis=0)
    mod = _adaln(cond, w_mod, b_mod).reshape(depth, COND_ROWS, N_MOD, d)

    ffn_w_in, ffn_w_out = ffn_w_in.astype(BF16), ffn_w_out.astype(BF16)
    cv_w1, cv_w2, dn_w_out = cv_w1.astype(BF16), cv_w2.astype(BF16), dn_w_out.astype(BF16)
    ffn_g, ffn_b = ln_g.reshape(depth, 3, 1, d), ln_b.reshape(depth, 3, 1, d)

    states = None
    for l in range(depth):
        x = _ffn(lay, x, mod, l, 0, ffn_w_in, ffn_w_out, ffn_g, ffn_b, alpha)
        j = l // N_MIXERS
        if l % N_MIXERS == 0:
            x = _conv_mixer(lay, x, mod, l, cv_w1, cv_b1[j], cv_dw[j], cv_dwb[j],
                            cv_ln_g[j], cv_ln_b[j], cv_w2, cv_b2[j], ln_g[l, 1], ln_b[l, 1], alpha)
        else:
            x, states = _dn_mixer(lay, x, mod, l, dn_w_in[j], dn_conv[j], dn_a_log[j], dn_dt_bias[j],
                                  dn_norm_g[j], dn_w_out, state_delta, states, ln_g[l, 1], ln_b[l, 1], alpha)
        x = _ffn(lay, x, mod, l, 1, ffn_w_in, ffn_w_out, ffn_g, ffn_b, alpha, split_out=(l == depth - 1))

    y_prompt, y_sample = x
    return (y_prompt.reshape(n_ctx_seq, ctx_len, d), y_sample.reshape(n_dec_seq, dec_len, d),
            states.astype(state_delta.dtype))
```

```python
import functools

import jax
import jax.numpy as jnp
from jax import lax
from jax.experimental import pallas as pl
from jax.experimental.pallas import tpu as pltpu

F32 = jnp.float32
BF16 = jnp.bfloat16

LN_EPS = 1e-5
L2_EPS = 1e-6
N_MOD = 9
N_MIXERS = 2
GRID_W = 64
DN_HEADS = 8
DN_DK = 128
DN_CHUNK = 64
COND_ROWS = 8
LANES = 128
VMEM_LIMIT = 56 * 1024 * 1024

TOK_TILE = 512
FFN_TILE = 1024
SEQ_BLOCK = 256
FF_CHUNK = 256


def _cparams(n_axes):
    return pltpu.CompilerParams(dimension_semantics=("arbitrary",) * n_axes,
                                vmem_limit_bytes=VMEM_LIMIT)


def _mm(a, b):
    return jnp.dot(a.astype(BF16), b.astype(BF16), preferred_element_type=F32)


def _mm_nt(a, b):
    return lax.dot_general(a.astype(BF16), b.astype(BF16), (((1,), (1,)), ((), ())),
                           preferred_element_type=F32)


def _mm_tn(a, b):
    return lax.dot_general(a.astype(BF16), b.astype(BF16), (((0,), (0,)), ((), ())),
                           preferred_element_type=F32)


def _sigmoid(x):
    return 1.0 / (1.0 + jnp.exp(-x))


def _silu(x):
    return x * _sigmoid(x)


def _layer_norm(r, g, b):
    mu = jnp.mean(r, axis=-1, keepdims=True)
    rc = r - mu
    var = jnp.mean(rc * rc, axis=-1, keepdims=True)
    return rc * lax.rsqrt(var + LN_EPS) * g + b


def _modulate(x, mod_ref, slot):
    shift = mod_ref[3 * slot:3 * slot + 1, :]
    scale = mod_ref[3 * slot + 1:3 * slot + 2, :]
    return x * (1.0 + scale) + shift


def _residual_norm(x, y, gate_scale, mod_ref, slot, g_ref, b_ref, alpha):
    gate = mod_ref[3 * slot + 2:3 * slot + 3, :]
    return _layer_norm(alpha * x + (gate_scale * gate) * y, g_ref[...], b_ref[...])


class _Layout:
    def __init__(self, n_ctx_seq, ctx_len, n_dec_seq, dec_len, d_model):
        self.n_ctx_seq, self.ctx_len = n_ctx_seq, ctx_len
        self.n_dec_seq, self.dec_len = n_dec_seq, dec_len
        self.n_ctx_tok = n_ctx_seq * ctx_len
        self.n_dec_tok = n_dec_seq * dec_len
        self.n_tok = self.n_ctx_tok + self.n_dec_tok
        self.d = d_model
        assert n_dec_seq + 1 <= COND_ROWS
        assert self.n_ctx_tok % TOK_TILE == 0 and dec_len % TOK_TILE == 0
        assert ctx_len == SEQ_BLOCK and dec_len % SEQ_BLOCK == 0
        assert dec_len % GRID_W == 0 and SEQ_BLOCK % GRID_W == 0

    def mod_row(self, i, tile):
        n_ctx_tiles = self.n_ctx_tok // tile
        per_seq = self.dec_len // tile
        return jnp.where(i < n_ctx_tiles, 0, 1 + (i - n_ctx_tiles) // per_seq)

    def mod_spec(self, layer, tile):
        return pl.BlockSpec((None, None, N_MOD, self.d),
                            lambda i: (layer, self.mod_row(i, tile), 0, 0))

    def group_specs(self, tile, width):
        n_ctx_tiles = self.n_ctx_tok // tile
        return (pl.BlockSpec((tile, width), lambda i: (jnp.minimum(i, n_ctx_tiles - 1), 0)),
                pl.BlockSpec((tile, width), lambda i: (jnp.maximum(i - n_ctx_tiles, 0), 0)))


def _row_spec(tile, width, col=0):
    return pl.BlockSpec((tile, width), lambda i: (i, col))


def _const_spec(shape):
    return pl.BlockSpec(shape, lambda *_: (0,) * len(shape))


def _stacked_spec(stacked, index):
    shape = stacked.shape[1:]
    return pl.BlockSpec((None,) + shape, lambda *_: (index,) + (0,) * len(shape))


def _adaln_kernel(cond_ref, w_ref, b_ref, o_ref):
    cnd = cond_ref[...]
    o_ref[...] = jnp.dot(_silu(cnd), w_ref[...], preferred_element_type=F32,
                         precision=lax.Precision.HIGHEST) + b_ref[...]


def _adaln(cond, w_mod, b_mod):
    depth, d, n_out = w_mod.shape
    bn = n_out // 4
    return pl.pallas_call(
        _adaln_kernel,
        out_shape=jax.ShapeDtypeStruct((depth, COND_ROWS, n_out), F32),
        grid=(depth, n_out // bn),
        in_specs=[pl.BlockSpec((COND_ROWS, d), lambda l, j: (0, 0)),
                  pl.BlockSpec((None, d, bn), lambda l, j: (l, 0, j)),
                  pl.BlockSpec((None, 1, bn), lambda l, j: (l, 0, j))],
        out_specs=pl.BlockSpec((None, COND_ROWS, bn), lambda l, j: (l, 0, j)),
        compiler_params=_cparams(2),
        name="adaln",
    )(cond, w_mod, b_mod.reshape(depth, 1, n_out))


def _ffn_kernel(*refs, slot, d_ff, alpha, n_ctx_tiles, split_in, split_out):
    n_in = 2 if split_in else 1
    n_out = 2 if split_out else 1
    x_refs, (mod_ref, w_in_ref, w_out_ref, g_ref, b_ref) = refs[:n_in], refs[n_in:n_in + 5]
    o_refs, act_ref = refs[n_in + 5:n_in + 5 + n_out], refs[-1]
    is_ctx = pl.program_id(0) < n_ctx_tiles
    x = jnp.where(is_ctx, x_refs[0][...], x_refs[1][...]) if split_in else x_refs[0][...]
    h = _modulate(x, mod_ref, slot).astype(BF16)
    for j in range(d_ff // FF_CHUNK):
        lo = j * FF_CHUNK
        lin = jnp.dot(h, w_in_ref[:, lo:lo + FF_CHUNK], preferred_element_type=F32)
        gat = jnp.dot(h, w_in_ref[:, d_ff + lo:d_ff + lo + FF_CHUNK], preferred_element_type=F32)
        act_ref[:, lo:lo + FF_CHUNK] = (_silu(gat) * lin).astype(BF16)
    y = jnp.dot(act_ref[...], w_out_ref[...], preferred_element_type=F32)
    out = _residual_norm(x, y, 0.5, mod_ref, slot, g_ref, b_ref, alpha)
    if split_out:
        @pl.when(is_ctx)
        def _():
            o_refs[0][...] = out

        @pl.when(jnp.logical_not(is_ctx))
        def _():
            o_refs[1][...] = out
    else:
        o_refs[0][...] = out


def _ffn(lay, xs, mod, layer, sub, w_in, w_out, g, b, alpha, split_out=False):
    d, d_ff = lay.d, w_out.shape[2]
    assert d_ff % FF_CHUNK == 0
    split_in = isinstance(xs, tuple)
    tile = FFN_TILE
    assert lay.n_ctx_tok % tile == 0 and lay.dec_len % tile == 0
    x_specs = list(lay.group_specs(tile, d)) if split_in else [_row_spec(tile, d)]
    if split_out:
        out_shape = (jax.ShapeDtypeStruct((lay.n_ctx_tok, d), F32), jax.ShapeDtypeStruct((lay.n_dec_tok, d), F32))
        out_specs = lay.group_specs(tile, d)
    else:
        out_shape = jax.ShapeDtypeStruct((lay.n_tok, d), F32)
        out_specs = _row_spec(tile, d)
    once = pl.Buffered(1)
    return pl.pallas_call(
        functools.partial(_ffn_kernel, slot=2 * sub, d_ff=d_ff, alpha=alpha,
                          n_ctx_tiles=lay.n_ctx_tok // tile, split_in=split_in, split_out=split_out),
        out_shape=out_shape,
        grid=(lay.n_tok // tile,),
        in_specs=x_specs + [lay.mod_spec(layer, tile),
                            pl.BlockSpec((None, None, d, 2 * d_ff), lambda i: (layer, sub, 0, 0),
                                         pipeline_mode=once),
                            pl.BlockSpec((None, None, d_ff, d), lambda i: (layer, sub, 0, 0),
                                         pipeline_mode=once),
                            pl.BlockSpec((None, None, 1, d), lambda i: (layer, 2 * sub, 0, 0)),
                            pl.BlockSpec((None, None, 1, d), lambda i: (layer, 2 * sub, 0, 0))],
        out_specs=out_specs,
        scratch_shapes=[pltpu.VMEM((tile, d_ff), BF16)],
        compiler_params=_cparams(1),
        name="ffn",
    )(*(xs if split_in else (xs,)), mod, w_in, w_out, g, b)


def _conv_in_kernel(x_ref, mod_ref, w1_ref, b1_ref, u_ref, *, d):
    h = _modulate(x_ref[...], mod_ref, 1).astype(BF16)
    lin = jnp.dot(h, w1_ref[:, :d], preferred_element_type=F32) + b1_ref[:, :d]
    gat = jnp.dot(h, w1_ref[:, d:], preferred_element_type=F32) + b1_ref[:, d:]
    u_ref[...] = lin * _sigmoid(gat)


def _conv_in(lay, x, mod, layer, w1, b1):
    d = lay.d
    return pl.pallas_call(
        functools.partial(_conv_in_kernel, d=d),
        out_shape=jax.ShapeDtypeStruct((lay.n_tok, d), F32),
        grid=(lay.n_tok // TOK_TILE,),
        in_specs=[_row_spec(TOK_TILE, d), lay.mod_spec(layer, TOK_TILE),
                  _stacked_spec(w1, layer // N_MIXERS), _const_spec((1, 2 * d))],
        out_specs=_row_spec(TOK_TILE, d),
        compiler_params=_cparams(1),
        name="conv_in",
    )(x, mod, w1, b1.reshape(1, 2 * d))


CONV_PAD = 16
CONV_ROWS = 64


SUBLANES = 8
DEC_CONV_CH = 256


def _segment_conv(u_ref, w_ref, o_ref, pad_ref, sh_ref, *, taps, seg_len, n_seg):
    n_ch = u_ref.shape[1]
    half = taps // 2
    n_sh = sh_ref.shape[1]
    zeros = jnp.zeros((CONV_PAD, n_ch), F32)
    pad_ref[0:CONV_PAD, :] = zeros
    pad_ref[CONV_PAD + seg_len:2 * CONV_PAD + seg_len, :] = zeros

    def segment(s, carry):
        row0 = pl.multiple_of(s * seg_len, seg_len)
        pad_ref[CONV_PAD:CONV_PAD + seg_len, :] = u_ref[pl.ds(row0, seg_len), :]

        def lane_tile(j, carry):
            lanes = pl.ds(pl.multiple_of(j * LANES, LANES), LANES)
            for p in range(1, SUBLANES):
                sh_ref[p] = pad_ref[p:p + n_sh, lanes]
            for r in range(seg_len // CONV_ROWS):
                acc = jnp.zeros((CONV_ROWS, LANES), F32)
                for k in range(taps):
                    off = CONV_PAD - half + k
                    p = off % SUBLANES
                    start = off - p + r * CONV_ROWS
                    src = (pad_ref[start:start + CONV_ROWS, lanes] if p == 0
                           else sh_ref[p, start:start + CONV_ROWS, :])
                    acc = acc + w_ref[k:k + 1, lanes] * src
                o_ref[pl.ds(row0 + r * CONV_ROWS, CONV_ROWS), lanes] = acc
            return carry

        return lax.fori_loop(0, n_ch // LANES, lane_tile, carry)

    lax.fori_loop(0, n_seg, segment, 0)


def _segment_conv_scratch(seg_len, n_ch):
    return [pltpu.VMEM((seg_len + 2 * CONV_PAD, n_ch), F32),
            pltpu.VMEM((SUBLANES, seg_len + 2 * CONV_PAD - SUBLANES, LANES), F32)]


def _ctx_conv_kernel(u_ref, w_ref, o_ref, pad_ref, sh_ref, *, taps):
    _segment_conv(u_ref, w_ref, o_ref, pad_ref, sh_ref, taps=taps, seg_len=u_ref.shape[0], n_seg=1)


def _ctx_conv(lay, u, w, taps):
    d = lay.d
    return pl.pallas_call(
        functools.partial(_ctx_conv_kernel, taps=taps),
        out_shape=jax.ShapeDtypeStruct((lay.n_ctx_tok, d), F32),
        grid=(lay.n_ctx_seq,),
        in_specs=[pl.BlockSpec((lay.ctx_len, d), lambda i: (i, 0)),
                  pl.BlockSpec((w.shape[0], d), lambda i: (0, 0))],
        out_specs=pl.BlockSpec((lay.ctx_len, d), lambda i: (i, 0)),
        scratch_shapes=_segment_conv_scratch(lay.ctx_len, d),
        compiler_params=_cparams(1),
        name="ctx_conv",
    )(u, w)


def _dec_conv_kernel(u_ref, w_ref, o_ref, pad_ref, sh_ref, vpad_ref, *, taps, n_row_blocks):
    j = pl.program_id(1)
    seq_len, n_ch = u_ref.shape

    @pl.when(j < n_row_blocks)
    def _():
        _segment_conv(u_ref, w_ref, o_ref, pad_ref, sh_ref, taps=taps, seg_len=GRID_W,
                      n_seg=seq_len // GRID_W)

    @pl.when(j >= n_row_blocks)
    def _():
        halo = (taps // 2) * GRID_W
        zeros = jnp.zeros((halo, n_ch), F32)
        vpad_ref[0:halo, :] = zeros
        vpad_ref[halo + seq_len:2 * halo + seq_len, :] = zeros
        vpad_ref[halo:halo + seq_len, :] = u_ref[...]

        def row_chunk(r, carry):
            base = pl.multiple_of(r * CONV_ROWS, CONV_ROWS)
            for t in range(n_ch // LANES):
                lanes = slice(t * LANES, (t + 1) * LANES)
                acc = jnp.zeros((CONV_ROWS, LANES), F32)
                for k in range(taps):
                    acc = acc + w_ref[k:k + 1, lanes] * vpad_ref[pl.ds(base + k * GRID_W, CONV_ROWS), lanes]
                o_ref[pl.ds(base, CONV_ROWS), lanes] = acc
            return carry

        lax.fori_loop(0, seq_len // CONV_ROWS, row_chunk, 0)


def _dec_conv(lay, u, w, taps):
    d = lay.d
    first_seq = lay.n_ctx_tok // lay.dec_len
    assert lay.n_ctx_tok % lay.dec_len == 0 and (d // 2) % DEC_CONV_CH == 0
    halo = (taps // 2) * GRID_W
    return pl.pallas_call(
        functools.partial(_dec_conv_kernel, taps=taps, n_row_blocks=(d // 2) // DEC_CONV_CH),
        out_shape=jax.ShapeDtypeStruct((lay.n_dec_tok, d), F32),
        grid=(lay.n_dec_seq, d // DEC_CONV_CH),
        in_specs=[pl.BlockSpec((lay.dec_len, DEC_CONV_CH), lambda b, j: (first_seq + b, j)),
                  pl.BlockSpec((w.shape[0], DEC_CONV_CH), lambda b, j: (0, j))],
        out_specs=pl.BlockSpec((lay.dec_len, DEC_CONV_CH), lambda b, j: (b, j)),
        scratch_shapes=_segment_conv_scratch(GRID_W, DEC_CONV_CH)
        + [pltpu.VMEM((lay.dec_len + 2 * halo, DEC_CONV_CH), F32)],
        compiler_params=_cparams(2),
        name="dec_conv",
    )(u, w)


def _conv_out_kernel(x_ref, cvc_ref, cvd_ref, mod_ref, dwb_ref, lng_ref, lnb_ref, w2_ref, b2_ref, g_ref, b_ref,
                     o_ref, *, alpha, n_ctx_tiles):
    cv = jnp.where(pl.program_id(0) < n_ctx_tiles, cvc_ref[...], cvd_ref[...])
    a = _silu(_layer_norm(cv + dwb_ref[...], lng_ref[...], lnb_ref[...]))
    y = jnp.dot(a.astype(BF16), w2_ref[...], preferred_element_type=F32) + b2_ref[...]
    o_ref[...] = _residual_norm(x_ref[...], y, 1.0, mod_ref, 1, g_ref, b_ref, alpha)


def _conv_out(lay, x, cv_ctx, cv_dec, mod, layer, dwb, lng, lnb, w2, b2, g, b, alpha):
    d = lay.d
    vec = lambda v: v.reshape(1, d)
    return pl.pallas_call(
        functools.partial(_conv_out_kernel, alpha=alpha, n_ctx_tiles=lay.n_ctx_tok // TOK_TILE),
        out_shape=jax.ShapeDtypeStruct((lay.n_tok, d), F32),
        grid=(lay.n_tok // TOK_TILE,),
        in_specs=[_row_spec(TOK_TILE, d), *lay.group_specs(TOK_TILE, d), lay.mod_spec(layer, TOK_TILE),
                  _const_spec((1, d)), _const_spec((1, d)), _const_spec((1, d)),
                  _stacked_spec(w2, layer // N_MIXERS), _const_spec((1, d)), _const_spec((1, d)),
                  _const_spec((1, d))],
        out_specs=_row_spec(TOK_TILE, d),
        compiler_params=_cparams(1),
        name="conv_out",
    )(x, cv_ctx, cv_dec, mod, vec(dwb), vec(lng), vec(lnb), w2, vec(b2), vec(g), vec(b))


def _conv_mixer(lay, x, mod, layer, w1, b1, dw, dwb, lng, lnb, w2, b2, g, b, alpha):
    taps = dw.shape[0]
    assert taps // 2 < CONV_PAD
    u = _conv_in(lay, x, mod, layer, w1, b1)
    dw_pad = jnp.pad(dw, ((0, -taps % SUBLANES), (0, 0)))
    cv_ctx = _ctx_conv(lay, u, dw_pad, taps)
    cv_dec = _dec_conv(lay, u, dw_pad, taps)
    return _conv_out(lay, x, cv_ctx, cv_dec, mod, layer, dwb, lng, lnb, w2, b2, g, b, alpha)


def _split3(x):
    hi = x.astype(BF16)
    r1 = x - hi.astype(F32)
    mid = r1.astype(BF16)
    lo = (r1 - mid.astype(F32)).astype(BF16)
    return hi, mid, lo


def _chunk_cumsum(tri, x):
    hi, mid, lo = _split3(x)
    dot = lambda t: jnp.dot(tri, t, preferred_element_type=F32)
    return dot(hi) + dot(mid) + dot(lo)


DN_HALO = 16
DN_COLS = 512


def _dn_in_kernel(x_ref, xp_ref, xn_ref, mod_ref, w_ref, wab_ref, cw_ref, alog_ref, dtb_ref,
                  qkv_ref, z_ref, gates_ref, h_ref, *, lay, n_qkv):
    i = pl.program_id(0)
    n_ctx_blocks = lay.n_ctx_tok // SEQ_BLOCK
    per_seq = lay.dec_len // SEQ_BLOCK
    pos = (i - n_ctx_blocks) % per_seq
    is_ctx = i < n_ctx_blocks
    has_prev = jnp.logical_not(is_ctx | (pos == 0))
    has_next = jnp.logical_not(is_ctx | (pos == per_seq - 1))
    lo, hi = DN_HALO, DN_HALO + SEQ_BLOCK
    h_ref[0:lo] = jnp.where(has_prev, _modulate(xp_ref[...], mod_ref, 1), 0.0).astype(BF16)
    h_ref[lo:hi] = _modulate(x_ref[...], mod_ref, 1).astype(BF16)
    h_ref[hi:hi + DN_HALO] = jnp.where(has_next, _modulate(xn_ref[...], mod_ref, 1), 0.0).astype(BF16)

    n_qk = 2 * DN_HEADS * DN_DK
    for j in range(n_qkv // DN_COLS):
        cols = slice(j * DN_COLS, (j + 1) * DN_COLS)
        pre = jnp.dot(h_ref[...], w_ref[:, cols], preferred_element_type=F32)
        n_rows = pre.shape[0]
        y = _silu(cw_ref[0:1, cols] * pltpu.roll(pre, 1, 0)[lo:hi] + cw_ref[1:2, cols] * pre[lo:hi]
                  + cw_ref[2:3, cols] * pltpu.roll(pre, n_rows - 1, 0)[lo:hi])
        for t in range(DN_COLS // DN_DK):
            c0 = j * DN_COLS + t * DN_DK
            yt = y[:, t * DN_DK:(t + 1) * DN_DK]
            if c0 < n_qk:
                inv = lax.rsqrt(jnp.sum(yt * yt, axis=-1, keepdims=True) + L2_EPS)
                yt = yt * (inv * (DN_DK ** -0.5 if c0 < n_qk // 2 else 1.0))
            qkv_ref[:, c0:c0 + DN_DK] = yt

    h = h_ref[lo:hi]
    z_ref[...] = jnp.dot(h, w_ref[:, n_qkv:], preferred_element_type=F32)
    ab = jnp.dot(h, wab_ref[...], preferred_element_type=F32)
    pre = ab + dtb_ref[...]
    softplus = jnp.maximum(pre, 0.0) + jnp.log(1.0 + jnp.exp(-jnp.abs(pre)))
    log_a = -jnp.exp(alog_ref[...]) * softplus
    beta = _sigmoid(ab)
    tile = ab.shape[0]
    row = lax.broadcasted_iota(jnp.int32, (tile, tile), 0)
    col = lax.broadcasted_iota(jnp.int32, (tile, tile), 1)
    same = (row // DN_CHUNK) == (col // DN_CHUNK)
    lower = (same & (row >= col)).astype(BF16)
    upper = (same & (row <= col)).astype(BF16)
    lane = lax.broadcasted_iota(jnp.int32, ab.shape, 1)
    g_fwd = _chunk_cumsum(lower, log_a)
    g_bwd = _chunk_cumsum(upper, log_a)
    gates_ref[...] = jnp.where(lane < DN_HEADS, g_fwd, jnp.where(lane < 2 * DN_HEADS, g_bwd, beta))


def _dn_in(lay, x, mod, layer, w_qkvz, w_ab, conv_w, alog_row, dtb_row, n_qkv):
    d, n_proj = w_qkvz.shape
    assert n_qkv % DN_COLS == 0 and SEQ_BLOCK % DN_HALO == 0
    sub = SEQ_BLOCK // DN_HALO
    n_units = lay.n_tok // DN_HALO
    return pl.pallas_call(
        functools.partial(_dn_in_kernel, lay=lay, n_qkv=n_qkv),
        out_shape=(jax.ShapeDtypeStruct((lay.n_tok, n_qkv), F32),
                   jax.ShapeDtypeStruct((lay.n_tok, n_proj - n_qkv), F32),
                   jax.ShapeDtypeStruct((lay.n_tok, LANES), F32)),
        grid=(lay.n_tok // SEQ_BLOCK,),
        in_specs=[_row_spec(SEQ_BLOCK, d),
                  pl.BlockSpec((DN_HALO, d), lambda i: (jnp.maximum(i * sub - 1, 0), 0)),
                  pl.BlockSpec((DN_HALO, d), lambda i: (jnp.minimum((i + 1) * sub, n_units - 1), 0)),
                  lay.mod_spec(layer, SEQ_BLOCK),
                  _const_spec((d, n_proj)), _const_spec((d, LANES)), _const_spec((SUBLANES, n_qkv)),
                  _const_spec((1, LANES)), _const_spec((1, LANES))],
        out_specs=(_row_spec(SEQ_BLOCK, n_qkv), _row_spec(SEQ_BLOCK, n_proj - n_qkv),
                   _row_spec(SEQ_BLOCK, LANES)),
        scratch_shapes=[pltpu.VMEM((SEQ_BLOCK + 2 * DN_HALO, d), BF16)],
        compiler_params=_cparams(1),
        name="dn_in",
    )(x, x, x, mod, w_qkvz, w_ab, conv_w, alog_row, dtb_row)


def _bmm(a, b):
    return jnp.einsum("bmk,bkn->bmn", a.astype(BF16), b.astype(BF16), preferred_element_type=F32)


def _bmm_nt(a, b):
    return jnp.einsum("bmk,bnk->bmn", a.astype(BF16), b.astype(BF16), preferred_element_type=F32)


def _bmm_tn(a, b):
    return jnp.einsum("bkm,bkn->bmn", a.astype(BF16), b.astype(BF16), preferred_element_type=F32)


def _chunk_masks():
    c = DN_CHUNK
    row = lax.broadcasted_iota(jnp.int32, (c, 2 * c), 0)
    col = lax.broadcasted_iota(jnp.int32, (c, 2 * c), 1) % c
    same = lambda n: (row // n) == (col // n)
    levels = []
    n = 8
    while n < c:
        levels.append(same(2 * n) & jnp.logical_not(same(n)))
        n *= 2
    return row, col, same(8), levels


def _block_diag(y):
    c = y.shape[1]
    lane = lax.broadcasted_iota(jnp.int32, y.shape[1:], 1)[None]
    zero = jnp.zeros((), y.dtype)
    return jnp.concatenate([jnp.where(lane < c, y, zero), jnp.where(lane >= c, y, zero)], axis=1)


def _pair_rows(x0, x1):
    zeros = jnp.zeros_like(x0)
    return jnp.concatenate([jnp.concatenate([x0, zeros], axis=2),
                            jnp.concatenate([zeros, x1], axis=2)], axis=1)


def _dir_where(mask_fwd, mask_bwd, x, other):
    half = x.shape[0] // 2
    return jnp.concatenate([jnp.where(mask_fwd[None], x[:half], other),
                            jnp.where(mask_bwd[None], x[half:], other)], axis=0)


def _unit_tri_inverse_minus_eye(a, base_mask, level_masks):
    bd = lambda y: _block_diag(y.astype(BF16))
    d = jnp.where(base_mask[None], a, 0.0)
    x = _bmm(d, bd(d))
    bd_x = bd(x)
    x2 = _bmm(x, bd_x)
    m1 = x - d - _bmm(d, bd_x)
    n = m1 + x2 + _bmm(m1, bd(x2))
    for mask in level_masks:
        e = jnp.where(mask[None], a, 0.0)
        y = e + _bmm(n, bd(e))
        n = n - y - _bmm(y, bd(n))
    return n


def _lane_broadcast(x, lane):
    lanes = lax.broadcasted_iota(jnp.int32, x.shape, 1)
    col = jnp.sum(jnp.where(lanes == lane, x, 0.0), axis=-1, keepdims=True)
    return jnp.broadcast_to(col, x.shape)


def _delta_kernel(*refs, lay, n_prev):
    if n_prev:
        (qf_ref, kf_ref, vf_ref, gf_ref, qb_ref, kb_ref, vb_ref, gb_ref, s0_ref, prev_ref,
         of_ref, ob_ref, sout_ref, s_ref) = refs
    else:
        (qf_ref, kf_ref, vf_ref, gf_ref, qb_ref, kb_ref, vb_ref, gb_ref, s0_ref,
         of_ref, ob_ref, sout_ref, s_ref) = refs
    i = pl.program_id(0)
    n_ctx_blocks = lay.n_ctx_tok // SEQ_BLOCK
    per_seq = lay.dec_len // SEQ_BLOCK
    nc, nh, c = SEQ_BLOCK // DN_CHUNK, DN_HEADS, DN_CHUNK
    npair = nh // 2
    assert nh % 2 == 0 and 2 * c == LANES and DN_DK == LANES
    heads = [2 * m + parity for parity in range(2) for m in range(npair)]
    is_ctx = i < n_ctx_blocks
    seq_start = is_ctx | ((i - n_ctx_blocks) % per_seq == 0)

    @pl.when(seq_start)
    def _():
        s0 = jnp.stack([s0_ref[d, h] for d in range(2) for h in heads])
        s_ref[...] = jnp.where(is_ctx, 0.0, s0)

    views = ((qf_ref, kf_ref, vf_ref, gf_ref), (qb_ref, kb_ref, vb_ref, gb_ref))
    lane = lax.broadcasted_iota(jnp.int32, (c, LANES), 1)
    qs, ks, vs, gs, bs = [], [], [], [], []
    q2s, k2s, g2s, b2s, gr2s = [], [], [], [], []
    for d, (q_ref, k_ref, v_ref, g_ref) in enumerate(views):
        gates_all = g_ref[...]
        gates_t = jnp.transpose(gates_all)
        for ci in range(nc):
            rows = slice(ci * c, (ci + 1) * c)
            gates = gates_all[rows]
            g_h = [_lane_broadcast(gates, d * nh + h) for h in range(nh)]
            b_h = [_lane_broadcast(gates, (2 + d) * nh + h) for h in range(nh)]
            for h in heads:
                lanes = slice(h * DN_DK, (h + 1) * DN_DK)
                qs.append(q_ref[rows, lanes])
                ks.append(k_ref[rows, lanes])
                vs.append(v_ref[rows, lanes])
                gs.append(g_h[h])
                bs.append(b_h[h])
            for m in range(npair):
                lanes2 = slice(2 * m * DN_DK, (2 * m + 2) * DN_DK)
                q2s.append(q_ref[rows, lanes2])
                k2s.append(k_ref[rows, lanes2])
                g2s.append(jnp.where(lane < c, g_h[2 * m], g_h[2 * m + 1]))
                b2s.append(jnp.where(lane < c, b_h[2 * m], b_h[2 * m + 1]))
                l0 = d * nh + 2 * m
                gr2s.append(jnp.concatenate([gates_t[l0:l0 + 1, rows], gates_t[l0 + 1:l0 + 2, rows]], axis=1))
    q, k, v = jnp.stack(qs), jnp.stack(ks), jnp.stack(vs)
    g, beta = jnp.stack(gs), jnp.stack(bs)
    q2, k2 = jnp.stack(q2s), jnp.stack(k2s)
    g2, beta2, g_row2 = jnp.stack(g2s), jnp.stack(b2s), jnp.stack(gr2s)
    half, half2 = nc * nh, nc * npair
    n_groups = 2 * nc

    row, col, base_mask, level_masks = _chunk_masks()
    lane2 = lax.broadcasted_iota(jnp.int32, (c, 2 * DN_DK), 1)[None]
    k_bd = jnp.concatenate([jnp.where(lane2 < DN_DK, k2, 0.0), jnp.where(lane2 >= DN_DK, k2, 0.0)], axis=1)
    qk_kk = _bmm_nt(jnp.concatenate([q2, k2], axis=1), k_bd)
    decay = jnp.exp(_dir_where(row >= col, row <= col, g2 - g_row2, -jnp.inf))
    a = beta2 * qk_kk[:, c:] * _dir_where(row > col, row < col, decay, 0.0)
    p = qk_kk[:, :c] * decay
    n = _unit_tri_inverse_minus_eye(a, base_mask, level_masks)
    eg = jnp.exp(g)
    g_last = jnp.concatenate([g[:half, c - 1:c], g[half:, 0:1]], axis=0)
    rhs = jnp.concatenate([beta * v, (beta * eg) * k], axis=2)
    width = rhs.shape[2]
    parity_slots = lambda x, par: jnp.concatenate(
        [x[grp * nh + par * npair:grp * nh + (par + 1) * npair] for grp in range(n_groups)], axis=0)
    n_rhs = _bmm(n, _pair_rows(parity_slots(rhs, 0), parity_slots(rhs, 1)))
    sol = rhs + jnp.concatenate(
        [n_rhs[grp * npair:(grp + 1) * npair, :, par * width:(par + 1) * width]
         for grp in range(n_groups) for par in range(2)], axis=0)
    u_t = sol[:, :, :DN_DK]
    wq = jnp.concatenate([sol[:, :, DN_DK:], q * eg], axis=1)
    kd = k * jnp.exp(g_last - g)
    gl = jnp.exp(g_last)

    s = s_ref[...]
    for ci in range(nc):
        cb = nc - 1 - ci
        pick = lambda x: jnp.concatenate([x[ci * nh:(ci + 1) * nh],
                                          x[half + cb * nh:half + (cb + 1) * nh]], axis=0)
        p_now = jnp.concatenate([p[ci * npair:(ci + 1) * npair],
                                 p[half2 + cb * npair:half2 + (cb + 1) * npair]], axis=0)
        ws = _bmm(pick(wq), s)
        u = pick(u_t) - ws[:, :c]
        u_bd = jnp.concatenate([_pair_rows(u[0:npair], u[npair:nh]),
                                _pair_rows(u[nh:nh + npair], u[nh + npair:])], axis=0)
        pu = _bmm(p_now, u_bd)
        o = ws[:, c:] + jnp.concatenate([pu[:npair, :, :DN_DK], pu[:npair, :, DN_DK:],
                                         pu[npair:, :, :DN_DK], pu[npair:, :, DN_DK:]], axis=0)
        s = pick(gl) * s + _bmm_tn(pick(kd), u)
        for slot, h in enumerate(heads):
            lanes = slice(h * DN_DK, (h + 1) * DN_DK)
            of_ref[ci * c:(ci + 1) * c, lanes] = o[slot]
            ob_ref[cb * c:(cb + 1) * c, lanes] = o[nh + slot]
    s_ref[...] = s

    @pl.when(is_ctx)
    def _():
        if n_prev:
            sout_ref[0:n_prev] = prev_ref[...]
        for d in range(2):
            for slot, h in enumerate(heads):
                sout_ref[n_prev, d, h] = s[d * nh + slot]


def _delta_rule(lay, qkv, gates, s0, layer_j, prev_states):
    n_prev = 0 if prev_states is None else prev_states.shape[1]
    assert n_prev == layer_j
    n_blocks = lay.n_tok // SEQ_BLOCK
    n_ctx_blocks = lay.n_ctx_tok // SEQ_BLOCK
    per_seq = lay.dec_len // SEQ_BLOCK
    vd = DN_HEADS * DN_DK

    def rev(i):
        j = i - n_ctx_blocks
        return jnp.where(i < n_ctx_blocks, i, n_ctx_blocks + (j // per_seq) * per_seq + per_seq - 1 - j % per_seq)

    fwd = lambda i: i
    cols = lambda blk, part: pl.BlockSpec((SEQ_BLOCK, vd), lambda i: (blk(i), part))
    gate_spec = lambda blk: pl.BlockSpec((SEQ_BLOCK, LANES), lambda i: (blk(i), 0))
    state_shape = (2, DN_HEADS, DN_DK, DN_DK)
    s0_spec = pl.BlockSpec((None, None) + state_shape,
                           lambda i: (jnp.maximum(i - n_ctx_blocks, 0) // per_seq, layer_j, 0, 0, 0, 0))
    ctx_states = lambda n: pl.BlockSpec((None, n) + state_shape,
                                        lambda i: (jnp.minimum(i, n_ctx_blocks - 1), 0, 0, 0, 0, 0))
    o_shape = jax.ShapeDtypeStruct((lay.n_tok, vd), F32)
    return pl.pallas_call(
        functools.partial(_delta_kernel, lay=lay, n_prev=n_prev),
        out_shape=(o_shape, o_shape,
                   jax.ShapeDtypeStruct((n_ctx_blocks, n_prev + 1) + state_shape, F32)),
        grid=(n_blocks,),
        in_specs=[cols(fwd, 0), cols(fwd, 1), cols(fwd, 2), gate_spec(fwd),
                  cols(rev, 0), cols(rev, 1), cols(rev, 2), gate_spec(rev), s0_spec]
        + ([ctx_states(n_prev)] if n_prev else []),
        out_specs=(cols(fwd, 0), cols(rev, 0), ctx_states(n_prev + 1)),
        scratch_shapes=[pltpu.VMEM((2 * DN_HEADS, DN_DK, DN_DK), F32)],
        compiler_params=_cparams(1),
        name="delta_rule",
    )(qkv, qkv, qkv, gates, qkv, qkv, qkv, gates, s0, *([prev_states] if n_prev else []))


def _dn_out_kernel(x_ref, of_ref, ob_ref, z_ref, mod_ref, ng_ref, w_ref, g_ref, b_ref, o_ref, act_ref,
                   *, alpha):
    for h in range(DN_HEADS):
        lanes = slice(h * DN_DK, (h + 1) * DN_DK)
        o = of_ref[:, lanes] + ob_ref[:, lanes]
        rms = lax.rsqrt(jnp.mean(o * o, axis=-1, keepdims=True) + LN_EPS)
        act_ref[:, lanes] = (o * rms * ng_ref[...] * _silu(z_ref[:, lanes])).astype(BF16)
    y = jnp.dot(act_ref[...], w_ref[...], preferred_element_type=F32)
    o_ref[...] = _residual_norm(x_ref[...], y, 1.0, mod_ref, 1, g_ref, b_ref, alpha)


def _dn_out(lay, x, o_f, o_b, z, mod, layer, norm_g, w_out, g, b, alpha):
    d = lay.d
    vd = DN_HEADS * DN_DK
    return pl.pallas_call(
        functools.partial(_dn_out_kernel, alpha=alpha),
        out_shape=jax.ShapeDtypeStruct((lay.n_tok, d), F32),
        grid=(lay.n_tok // TOK_TILE,),
        in_specs=[_row_spec(TOK_TILE, d), _row_spec(TOK_TILE, vd), _row_spec(TOK_TILE, vd),
                  _row_spec(TOK_TILE, vd), lay.mod_spec(layer, TOK_TILE),
                  _const_spec((1, DN_DK)), _stacked_spec(w_out, layer // N_MIXERS), _const_spec((1, d)),
                  _const_spec((1, d))],
        out_specs=_row_spec(TOK_TILE, d),
        scratch_shapes=[pltpu.VMEM((TOK_TILE, vd), BF16)],
        compiler_params=_cparams(1),
        name="dn_out",
    )(x, o_f, o_b, z, mod, norm_g.reshape(1, DN_DK), w_out, g.reshape(1, d), b.reshape(1, d))


def _dn_mixer(lay, x, mod, layer, w_in, conv_w, a_log, dt_bias, norm_g, w_out, s0, prev_states, g, b, alpha):
    layer_j = layer // N_MIXERS
    qk, vd = DN_HEADS * DN_DK, DN_HEADS * DN_DK
    n_qkv = 2 * qk + vd
    n_gate = 4 * DN_HEADS
    assert w_in.shape[1] == n_qkv + vd + n_gate and n_gate <= LANES
    w_qkvz = w_in[:, :n_qkv + vd].astype(BF16)
    w_ab = jnp.pad(w_in[:, n_qkv + vd:], ((0, 0), (0, LANES - n_gate))).astype(BF16)
    lane_row = lambda v: jnp.pad(v.reshape(1, 2 * DN_HEADS), ((0, 0), (0, LANES - 2 * DN_HEADS)))
    assert conv_w.shape[0] == 3
    conv_pad = jnp.pad(conv_w, ((0, SUBLANES - conv_w.shape[0]), (0, 0)))
    qkv, z, gates = _dn_in(lay, x, mod, layer, w_qkvz, w_ab, conv_pad, lane_row(a_log), lane_row(dt_bias), n_qkv)
    o_f, o_b, s_fin = _delta_rule(lay, qkv, gates, s0, layer_j, prev_states)
    x = _dn_out(lay, x, o_f, o_b, z, mod, layer, norm_g, w_out, g, b, alpha)
    return x, s_fin


def kernel(x_prompt, x_sample, state_delta, c, c_ctx, w_mod, b_mod, ln_g, ln_b, ffn_w_in, ffn_w_out,
           cv_w1, cv_b1, cv_dw, cv_dwb, cv_ln_g, cv_ln_b, cv_w2, cv_b2,
           dn_w_in, dn_conv, dn_a_log, dn_dt_bias, dn_norm_g, dn_w_out):
    n_ctx_seq, ctx_len, d = x_prompt.shape
    n_dec_seq, dec_len, _ = x_sample.shape
    depth = w_mod.shape[0]
    alpha = (2.0 * depth) ** 0.25
    lay = _Layout(n_ctx_seq, ctx_len, n_dec_seq, dec_len, d)

    x = (x_prompt.reshape(lay.n_ctx_tok, d), x_sample.reshape(lay.n_dec_tok, d))
    cond = jnp.concatenate([c_ctx[None, :], c, jnp.zeros((COND_ROWS - 1 - n_dec_seq, d), F32)], axis=0)
    mod = _adaln(cond, w_mod, b_mod).reshape(depth, COND_ROWS, N_MOD, d)

    ffn_w_in, ffn_w_out = ffn_w_in.astype(BF16), ffn_w_out.astype(BF16)
    cv_w1, cv_w2, dn_w_out = cv_w1.astype(BF16), cv_w2.astype(BF16), dn_w_out.astype(BF16)
    ffn_g, ffn_b = ln_g.reshape(depth, 3, 1, d), ln_b.reshape(depth, 3, 1, d)

    states = None
    for l in range(depth):
        x = _ffn(lay, x, mod, l, 0, ffn_w_in, ffn_w_out, ffn_g, ffn_b, alpha)
        j = l // N_MIXERS
        if l % N_MIXERS == 0:
            x = _conv_mixer(lay, x, mod, l, cv_w1, cv_b1[j], cv_dw[j], cv_dwb[j],
                            cv_ln_g[j], cv_ln_b[j], cv_w2, cv_b2[j], ln_g[l, 1], ln_b[l, 1], alpha)
        else:
            x, states = _dn_mixer(lay, x, mod, l, dn_w_in[j], dn_conv[j], dn_a_log[j], dn_dt_bias[j],
                                  dn_norm_g[j], dn_w_out, state_delta, states, ln_g[l, 1], ln_b[l, 1], alpha)
        x = _ffn(lay, x, mod, l, 1, ffn_w_in, ffn_w_out, ffn_g, ffn_b, alpha, split_out=(l == depth - 1))

    y_prompt, y_sample = x
    return (y_prompt.reshape(n_ctx_seq, ctx_len, d), y_sample.reshape(n_dec_seq, dec_len, d),
            states.astype(state_delta.dtype))
```

```python
import functools

import jax
import jax.numpy as jnp
from jax import lax
from jax.experimental import pallas as pl
from jax.experimental.pallas import tpu as pltpu

F32 = jnp.float32
BF16 = jnp.bfloat16

LN_EPS = 1e-5
L2_EPS = 1e-6
N_MOD = 9
N_MIXERS = 2
GRID_W = 64
DN_HEADS = 8
DN_DK = 128
DN_CHUNK = 64
COND_ROWS = 8
LANES = 128
VMEM_LIMIT = 56 * 1024 * 1024

TOK_TILE = 512
FFN_TILE = 1024
FFN_ROWS = 256
SEQ_BLOCK = 256
FF_CHUNK = 256


def _cparams(n_axes):
    return pltpu.CompilerParams(dimension_semantics=("arbitrary",) * n_axes,
                                vmem_limit_bytes=VMEM_LIMIT)


def _mm(a, b):
    return jnp.dot(a.astype(BF16), b.astype(BF16), preferred_element_type=F32)


def _mm_nt(a, b):
    return lax.dot_general(a.astype(BF16), b.astype(BF16), (((1,), (1,)), ((), ())),
                           preferred_element_type=F32)


def _mm_tn(a, b):
    return lax.dot_general(a.astype(BF16), b.astype(BF16), (((0,), (0,)), ((), ())),
                           preferred_element_type=F32)


def _sigmoid(x):
    return 1.0 / (1.0 + jnp.exp(-x))


def _silu(x):
    return x * _sigmoid(x)


def _layer_norm(r, g, b):
    mu = jnp.mean(r, axis=-1, keepdims=True)
    rc = r - mu
    var = jnp.mean(rc * rc, axis=-1, keepdims=True)
    return rc * lax.rsqrt(var + LN_EPS) * g + b


def _modulate(x, mod_ref, slot):
    shift = mod_ref[3 * slot:3 * slot + 1, :]
    scale = mod_ref[3 * slot + 1:3 * slot + 2, :]
    return x * (1.0 + scale) + shift


def _residual_norm(x, y, gate_scale, mod_ref, slot, g_ref, b_ref, alpha):
    gate = mod_ref[3 * slot + 2:3 * slot + 3, :]
    return _layer_norm(alpha * x + (gate_scale * gate) * y, g_ref[...], b_ref[...])


class _Layout:
    def __init__(self, n_ctx_seq, ctx_len, n_dec_seq, dec_len, d_model):
        self.n_ctx_seq, self.ctx_len = n_ctx_seq, ctx_len
        self.n_dec_seq, self.dec_len = n_dec_seq, dec_len
        self.n_ctx_tok = n_ctx_seq * ctx_len
        self.n_dec_tok = n_dec_seq * dec_len
        self.n_tok = self.n_ctx_tok + self.n_dec_tok
        self.d = d_model
        assert n_dec_seq + 1 <= COND_ROWS
        assert self.n_ctx_tok % TOK_TILE == 0 and dec_len % TOK_TILE == 0
        assert ctx_len == SEQ_BLOCK and dec_len % SEQ_BLOCK == 0
        assert dec_len % GRID_W == 0 and SEQ_BLOCK % GRID_W == 0

    def mod_row(self, i, tile):
        n_ctx_tiles = self.n_ctx_tok // tile
        per_seq = self.dec_len // tile
        return jnp.where(i < n_ctx_tiles, 0, 1 + (i - n_ctx_tiles) // per_seq)

    def mod_spec(self, layer, tile):
        return pl.BlockSpec((None, None, N_MOD, self.d),
                            lambda i: (layer, self.mod_row(i, tile), 0, 0))

    def group_specs(self, tile, width):
        n_ctx_tiles = self.n_ctx_tok // tile
        return (pl.BlockSpec((tile, width), lambda i: (jnp.minimum(i, n_ctx_tiles - 1), 0)),
                pl.BlockSpec((tile, width), lambda i: (jnp.maximum(i - n_ctx_tiles, 0), 0)))


def _row_spec(tile, width, col=0):
    return pl.BlockSpec((tile, width), lambda i: (i, col))


def _const_spec(shape):
    return pl.BlockSpec(shape, lambda *_: (0,) * len(shape))


def _stacked_spec(stacked, index):
    shape = stacked.shape[1:]
    return pl.BlockSpec((None,) + shape, lambda *_: (index,) + (0,) * len(shape))


def _adaln_kernel(cond_ref, w_ref, b_ref, o_ref):
    cnd = cond_ref[...]
    o_ref[...] = jnp.dot(_silu(cnd), w_ref[...], preferred_element_type=F32,
                         precision=lax.Precision.HIGHEST) + b_ref[...]


def _adaln(cond, w_mod, b_mod):
    depth, d, n_out = w_mod.shape
    bn = n_out // 4
    return pl.pallas_call(
        _adaln_kernel,
        out_shape=jax.ShapeDtypeStruct((depth, COND_ROWS, n_out), F32),
        grid=(depth, n_out // bn),
        in_specs=[pl.BlockSpec((COND_ROWS, d), lambda l, j: (0, 0)),
                  pl.BlockSpec((None, d, bn), lambda l, j: (l, 0, j)),
                  pl.BlockSpec((None, 1, bn), lambda l, j: (l, 0, j))],
        out_specs=pl.BlockSpec((None, COND_ROWS, bn), lambda l, j: (l, 0, j)),
        compiler_params=_cparams(2),
        name="adaln",
    )(cond, w_mod, b_mod.reshape(depth, 1, n_out))


def _ffn_kernel(*refs, slot, d_ff, alpha, n_ctx_tiles, split_in, split_out):
    n_in = 2 if split_in else 1
    n_out = 2 if split_out else 1
    x_refs, (mod_ref, w_in_ref, w_out_ref, g_ref, b_ref) = refs[:n_in], refs[n_in:n_in + 5]
    o_refs, act_ref = refs[n_in + 5:n_in + 5 + n_out], refs[-1]
    is_ctx = pl.program_id(0) < n_ctx_tiles
    outs = []
    for r in range(act_ref.shape[0] // FFN_ROWS):
        rows = slice(r * FFN_ROWS, (r + 1) * FFN_ROWS)
        x = jnp.where(is_ctx, x_refs[0][rows], x_refs[1][rows]) if split_in else x_refs[0][rows]
        h = _modulate(x, mod_ref, slot).astype(BF16)
        for j in range(d_ff // FF_CHUNK):
            lo = j * FF_CHUNK
            lin = jnp.dot(h, w_in_ref[:, lo:lo + FF_CHUNK], preferred_element_type=F32)
            gat = jnp.dot(h, w_in_ref[:, d_ff + lo:d_ff + lo + FF_CHUNK], preferred_element_type=F32)
            act_ref[rows, lo:lo + FF_CHUNK] = (_silu(gat) * lin).astype(BF16)
        y = jnp.dot(act_ref[rows, :], w_out_ref[...], preferred_element_type=F32)
        out = _residual_norm(x, y, 0.5, mod_ref, slot, g_ref, b_ref, alpha)
        if split_out:
            outs.append(out)
        else:
            o_refs[0][rows] = out
    if split_out:
        out = jnp.concatenate(outs, axis=0)

        @pl.when(is_ctx)
        def _():
            o_refs[0][...] = out

        @pl.when(jnp.logical_not(is_ctx))
        def _():
            o_refs[1][...] = out


def _ffn(lay, xs, mod, layer, sub, w_in, w_out, g, b, alpha, split_out=False):
    d, d_ff = lay.d, w_out.shape[2]
    assert d_ff % FF_CHUNK == 0
    split_in = isinstance(xs, tuple)
    tile = FFN_TILE
    assert lay.n_ctx_tok % tile == 0 and lay.dec_len % tile == 0
    x_specs = list(lay.group_specs(tile, d)) if split_in else [_row_spec(tile, d)]
    if split_out:
        out_shape = (jax.ShapeDtypeStruct((lay.n_ctx_tok, d), F32), jax.ShapeDtypeStruct((lay.n_dec_tok, d), F32))
        out_specs = lay.group_specs(tile, d)
    else:
        out_shape = jax.ShapeDtypeStruct((lay.n_tok, d), F32)
        out_specs = _row_spec(tile, d)
    once = pl.Buffered(1)
    return pl.pallas_call(
        functools.partial(_ffn_kernel, slot=2 * sub, d_ff=d_ff, alpha=alpha,
                          n_ctx_tiles=lay.n_ctx_tok // tile, split_in=split_in, split_out=split_out),
        out_shape=out_shape,
        grid=(lay.n_tok // tile,),
        in_specs=x_specs + [lay.mod_spec(layer, tile),
                            pl.BlockSpec((None, None, d, 2 * d_ff), lambda i: (layer, sub, 0, 0),
                                         pipeline_mode=once),
                            pl.BlockSpec((None, None, d_ff, d), lambda i: (layer, sub, 0, 0),
                                         pipeline_mode=once),
                            pl.BlockSpec((None, None, 1, d), lambda i: (layer, 2 * sub, 0, 0)),
                            pl.BlockSpec((None, None, 1, d), lambda i: (layer, 2 * sub, 0, 0))],
        out_specs=out_specs,
        scratch_shapes=[pltpu.VMEM((tile, d_ff), BF16)],
        compiler_params=_cparams(1),
        name="ffn",
    )(*(xs if split_in else (xs,)), mod, w_in, w_out, g, b)


def _conv_in_kernel(x_ref, mod_ref, w1_ref, b1_ref, u_ref, *, d):
    h = _modulate(x_ref[...], mod_ref, 1).astype(BF16)
    lin = jnp.dot(h, w1_ref[:, :d], preferred_element_type=F32) + b1_ref[:, :d]
    gat = jnp.dot(h, w1_ref[:, d:], preferred_element_type=F32) + b1_ref[:, d:]
    u_ref[...] = lin * _sigmoid(gat)


def _conv_in(lay, x, mod, layer, w1, b1):
    d = lay.d
    return pl.pallas_call(
        functools.partial(_conv_in_kernel, d=d),
        out_shape=jax.ShapeDtypeStruct((lay.n_tok, d), F32),
        grid=(lay.n_tok // TOK_TILE,),
        in_specs=[_row_spec(TOK_TILE, d), lay.mod_spec(layer, TOK_TILE),
                  _stacked_spec(w1, layer // N_MIXERS), _const_spec((1, 2 * d))],
        out_specs=_row_spec(TOK_TILE, d),
        compiler_params=_cparams(1),
        name="conv_in",
    )(x, mod, w1, b1.reshape(1, 2 * d))


CONV_PAD = 16
CONV_ROWS = 64


SUBLANES = 8
DEC_CONV_CH = 256


def _segment_conv(u_ref, w_ref, o_ref, pad_ref, sh_ref, *, taps, seg_len, n_seg):
    n_ch = u_ref.shape[1]
    half = taps // 2
    n_sh = sh_ref.shape[1]
    zeros = jnp.zeros((CONV_PAD, n_ch), F32)
    pad_ref[0:CONV_PAD, :] = zeros
    pad_ref[CONV_PAD + seg_len:2 * CONV_PAD + seg_len, :] = zeros

    def segment(s, carry):
        row0 = pl.multiple_of(s * seg_len, seg_len)
        pad_ref[CONV_PAD:CONV_PAD + seg_len, :] = u_ref[pl.ds(row0, seg_len), :]

        def lane_tile(j, carry):
            lanes = pl.ds(pl.multiple_of(j * LANES, LANES), LANES)
            for p in range(1, SUBLANES):
                sh_ref[p] = pad_ref[p:p + n_sh, lanes]
            for r in range(seg_len // CONV_ROWS):
                acc = jnp.zeros((CONV_ROWS, LANES), F32)
                for k in range(taps):
                    off = CONV_PAD - half + k
                    p = off % SUBLANES
                    start = off - p + r * CONV_ROWS
                    src = (pad_ref[start:start + CONV_ROWS, lanes] if p == 0
                           else sh_ref[p, start:start + CONV_ROWS, :])
                    acc = acc + w_ref[k:k + 1, lanes] * src
                o_ref[pl.ds(row0 + r * CONV_ROWS, CONV_ROWS), lanes] = acc
            return carry

        return lax.fori_loop(0, n_ch // LANES, lane_tile, carry)

    lax.fori_loop(0, n_seg, segment, 0)


def _segment_conv_scratch(seg_len, n_ch):
    return [pltpu.VMEM((seg_len + 2 * CONV_PAD, n_ch), F32),
            pltpu.VMEM((SUBLANES, seg_len + 2 * CONV_PAD - SUBLANES, LANES), F32)]


def _ctx_conv_kernel(u_ref, w_ref, o_ref, pad_ref, sh_ref, *, taps):
    _segment_conv(u_ref, w_ref, o_ref, pad_ref, sh_ref, taps=taps, seg_len=u_ref.shape[0], n_seg=1)


def _ctx_conv(lay, u, w, taps):
    d = lay.d
    return pl.pallas_call(
        functools.partial(_ctx_conv_kernel, taps=taps),
        out_shape=jax.ShapeDtypeStruct((lay.n_ctx_tok, d), F32),
        grid=(lay.n_ctx_seq,),
        in_specs=[pl.BlockSpec((lay.ctx_len, d), lambda i: (i, 0)),
                  pl.BlockSpec((w.shape[0], d), lambda i: (0, 0))],
        out_specs=pl.BlockSpec((lay.ctx_len, d), lambda i: (i, 0)),
        scratch_shapes=_segment_conv_scratch(lay.ctx_len, d),
        compiler_params=_cparams(1),
        name="ctx_conv",
    )(u, w)


def _dec_conv_kernel(u_ref, w_ref, o_ref, pad_ref, sh_ref, vpad_ref, *, taps, n_row_blocks):
    j = pl.program_id(1)
    seq_len, n_ch = u_ref.shape

    @pl.when(j < n_row_blocks)
    def _():
        _segment_conv(u_ref, w_ref, o_ref, pad_ref, sh_ref, taps=taps, seg_len=GRID_W,
                      n_seg=seq_len // GRID_W)

    @pl.when(j >= n_row_blocks)
    def _():
        halo = (taps // 2) * GRID_W
        zeros = jnp.zeros((halo, n_ch), F32)
        vpad_ref[0:halo, :] = zeros
        vpad_ref[halo + seq_len:2 * halo + seq_len, :] = zeros
        vpad_ref[halo:halo + seq_len, :] = u_ref[...]

        def row_chunk(r, carry):
            base = pl.multiple_of(r * CONV_ROWS, CONV_ROWS)
            for t in range(n_ch // LANES):
                lanes = slice(t * LANES, (t + 1) * LANES)
                acc = jnp.zeros((CONV_ROWS, LANES), F32)
                for k in range(taps):
                    acc = acc + w_ref[k:k + 1, lanes] * vpad_ref[pl.ds(base + k * GRID_W, CONV_ROWS), lanes]
                o_ref[pl.ds(base, CONV_ROWS), lanes] = acc
            return carry

        lax.fori_loop(0, seq_len // CONV_ROWS, row_chunk, 0)


def _dec_conv(lay, u, w, taps):
    d = lay.d
    first_seq = lay.n_ctx_tok // lay.dec_len
    assert lay.n_ctx_tok % lay.dec_len == 0 and (d // 2) % DEC_CONV_CH == 0
    halo = (taps // 2) * GRID_W
    return pl.pallas_call(
        functools.partial(_dec_conv_kernel, taps=taps, n_row_blocks=(d // 2) // DEC_CONV_CH),
        out_shape=jax.ShapeDtypeStruct((lay.n_dec_tok, d), F32),
        grid=(lay.n_dec_seq, d // DEC_CONV_CH),
        in_specs=[pl.BlockSpec((lay.dec_len, DEC_CONV_CH), lambda b, j: (first_seq + b, j)),
                  pl.BlockSpec((w.shape[0], DEC_CONV_CH), lambda b, j: (0, j))],
        out_specs=pl.BlockSpec((lay.dec_len, DEC_CONV_CH), lambda b, j: (b, j)),
        scratch_shapes=_segment_conv_scratch(GRID_W, DEC_CONV_CH)
        + [pltpu.VMEM((lay.dec_len + 2 * halo, DEC_CONV_CH), F32)],
        compiler_params=_cparams(2),
        name="dec_conv",
    )(u, w)


def _conv_out_kernel(x_ref, cvc_ref, cvd_ref, mod_ref, dwb_ref, lng_ref, lnb_ref, w2_ref, b2_ref, g_ref, b_ref,
                     o_ref, *, alpha, n_ctx_tiles):
    cv = jnp.where(pl.program_id(0) < n_ctx_tiles, cvc_ref[...], cvd_ref[...])
    a = _silu(_layer_norm(cv + dwb_ref[...], lng_ref[...], lnb_ref[...]))
    y = jnp.dot(a.astype(BF16), w2_ref[...], preferred_element_type=F32) + b2_ref[...]
    o_ref[...] = _residual_norm(x_ref[...], y, 1.0, mod_ref, 1, g_ref, b_ref, alpha)


def _conv_out(lay, x, cv_ctx, cv_dec, mod, layer, dwb, lng, lnb, w2, b2, g, b, alpha):
    d = lay.d
    vec = lambda v: v.reshape(1, d)
    return pl.pallas_call(
        functools.partial(_conv_out_kernel, alpha=alpha, n_ctx_tiles=lay.n_ctx_tok // TOK_TILE),
        out_shape=jax.ShapeDtypeStruct((lay.n_tok, d), F32),
        grid=(lay.n_tok // TOK_TILE,),
        in_specs=[_row_spec(TOK_TILE, d), *lay.group_specs(TOK_TILE, d), lay.mod_spec(layer, TOK_TILE),
                  _const_spec((1, d)), _const_spec((1, d)), _const_spec((1, d)),
                  _stacked_spec(w2, layer // N_MIXERS), _const_spec((1, d)), _const_spec((1, d)),
                  _const_spec((1, d))],
        out_specs=_row_spec(TOK_TILE, d),
        compiler_params=_cparams(1),
        name="conv_out",
    )(x, cv_ctx, cv_dec, mod, vec(dwb), vec(lng), vec(lnb), w2, vec(b2), vec(g), vec(b))


def _conv_mixer(lay, x, mod, layer, w1, b1, dw, dwb, lng, lnb, w2, b2, g, b, alpha):
    taps = dw.shape[0]
    assert taps // 2 < CONV_PAD
    u = _conv_in(lay, x, mod, layer, w1, b1)
    dw_pad = jnp.pad(dw, ((0, -taps % SUBLANES), (0, 0)))
    cv_ctx = _ctx_conv(lay, u, dw_pad, taps)
    cv_dec = _dec_conv(lay, u, dw_pad, taps)
    return _conv_out(lay, x, cv_ctx, cv_dec, mod, layer, dwb, lng, lnb, w2, b2, g, b, alpha)


def _split3(x):
    hi = x.astype(BF16)
    r1 = x - hi.astype(F32)
    mid = r1.astype(BF16)
    lo = (r1 - mid.astype(F32)).astype(BF16)
    return hi, mid, lo


def _chunk_cumsum(tri, x):
    hi, mid, lo = _split3(x)
    dot = lambda t: jnp.dot(tri, t, preferred_element_type=F32)
    return dot(hi) + dot(mid) + dot(lo)


DN_HALO = 16
DN_COLS = 256


def _dn_in_kernel(x_ref, xp_ref, xn_ref, mod_ref, w_ref, wab_ref, cw_ref, alog_ref, dtb_ref,
                  qkv_ref, z_ref, gates_ref, h_ref, *, lay, n_qkv):
    i = pl.program_id(0)
    n_ctx_blocks = lay.n_ctx_tok // SEQ_BLOCK
    per_seq = lay.dec_len // SEQ_BLOCK
    pos = (i - n_ctx_blocks) % per_seq
    is_ctx = i < n_ctx_blocks
    has_prev = jnp.logical_not(is_ctx | (pos == 0))
    has_next = jnp.logical_not(is_ctx | (pos == per_seq - 1))
    lo, hi = DN_HALO, DN_HALO + SEQ_BLOCK
    h_ref[0:lo] = jnp.where(has_prev, _modulate(xp_ref[...], mod_ref, 1), 0.0).astype(BF16)
    h_ref[lo:hi] = _modulate(x_ref[...], mod_ref, 1).astype(BF16)
    h_ref[hi:hi + DN_HALO] = jnp.where(has_next, _modulate(xn_ref[...], mod_ref, 1), 0.0).astype(BF16)

    n_qk = 2 * DN_HEADS * DN_DK
    n_rows = h_ref.shape[0]
    for j in range(n_qkv // DN_COLS):
        cols = slice(j * DN_COLS, (j + 1) * DN_COLS)
        pre = jnp.dot(h_ref[...], w_ref[:, cols], preferred_element_type=F32)
        y = _silu(cw_ref[0:1, cols] * pltpu.roll(pre, 1, 0)[lo:hi] + cw_ref[1:2, cols] * pre[lo:hi]
                  + cw_ref[2:3, cols] * pltpu.roll(pre, n_rows - 1, 0)[lo:hi])
        for t in range(DN_COLS // DN_DK):
            c0 = j * DN_COLS + t * DN_DK
            yt = y[:, t * DN_DK:(t + 1) * DN_DK]
            if c0 < n_qk:
                inv = lax.rsqrt(jnp.sum(yt * yt, axis=-1, keepdims=True) + L2_EPS)
                yt = yt * (inv * (DN_DK ** -0.5 if c0 < n_qk // 2 else 1.0))
            qkv_ref[:, c0:c0 + DN_DK] = yt

    h = h_ref[lo:hi]
    z_ref[...] = jnp.dot(h, w_ref[:, n_qkv:], preferred_element_type=F32)
    ab = jnp.dot(h, wab_ref[...], preferred_element_type=F32)
    pre = ab + dtb_ref[...]
    softplus = jnp.maximum(pre, 0.0) + jnp.log(1.0 + jnp.exp(-jnp.abs(pre)))
    log_a = -jnp.exp(alog_ref[...]) * softplus
    beta = _sigmoid(ab)
    tile = ab.shape[0]
    row = lax.broadcasted_iota(jnp.int32, (tile, tile), 0)
    col = lax.broadcasted_iota(jnp.int32, (tile, tile), 1)
    same = (row // DN_CHUNK) == (col // DN_CHUNK)
    lower = (same & (row >= col)).astype(BF16)
    upper = (same & (row <= col)).astype(BF16)
    lane = lax.broadcasted_iota(jnp.int32, ab.shape, 1)
    g_fwd = _chunk_cumsum(lower, log_a)
    g_bwd = _chunk_cumsum(upper, log_a)
    gates_ref[...] = jnp.where(lane < DN_HEADS, g_fwd, jnp.where(lane < 2 * DN_HEADS, g_bwd, beta))


def _dn_in(lay, x, mod, layer, w_qkvz, w_ab, conv_w, alog_row, dtb_row, n_qkv):
    d, n_proj = w_qkvz.shape
    assert n_qkv % DN_COLS == 0 and SEQ_BLOCK % DN_HALO == 0
    sub = SEQ_BLOCK // DN_HALO
    n_units = lay.n_tok // DN_HALO
    return pl.pallas_call(
        functools.partial(_dn_in_kernel, lay=lay, n_qkv=n_qkv),
        out_shape=(jax.ShapeDtypeStruct((lay.n_tok, n_qkv), F32),
                   jax.ShapeDtypeStruct((lay.n_tok, n_proj - n_qkv), F32),
                   jax.ShapeDtypeStruct((lay.n_tok, LANES), F32)),
        grid=(lay.n_tok // SEQ_BLOCK,),
        in_specs=[_row_spec(SEQ_BLOCK, d),
                  pl.BlockSpec((DN_HALO, d), lambda i: (jnp.maximum(i * sub - 1, 0), 0)),
                  pl.BlockSpec((DN_HALO, d), lambda i: (jnp.minimum((i + 1) * sub, n_units - 1), 0)),
                  lay.mod_spec(layer, SEQ_BLOCK),
                  _const_spec((d, n_proj)), _const_spec((d, LANES)), _const_spec((SUBLANES, n_qkv)),
                  _const_spec((1, LANES)), _const_spec((1, LANES))],
        out_specs=(_row_spec(SEQ_BLOCK, n_qkv), _row_spec(SEQ_BLOCK, n_proj - n_qkv),
                   _row_spec(SEQ_BLOCK, LANES)),
        scratch_shapes=[pltpu.VMEM((SEQ_BLOCK + 2 * DN_HALO, d), BF16)],
        compiler_params=_cparams(1),
        name="dn_in",
    )(x, x, x, mod, w_qkvz, w_ab, conv_w, alog_row, dtb_row)


def _bmm(a, b):
    return jnp.einsum("bmk,bkn->bmn", a.astype(BF16), b.astype(BF16), preferred_element_type=F32)


def _bmm_nt(a, b):
    return jnp.einsum("bmk,bnk->bmn", a.astype(BF16), b.astype(BF16), preferred_element_type=F32)


def _bmm_tn(a, b):
    return jnp.einsum("bkm,bkn->bmn", a.astype(BF16), b.astype(BF16), preferred_element_type=F32)


def _chunk_masks():
    c = DN_CHUNK
    row = lax.broadcasted_iota(jnp.int32, (c, 2 * c), 0)
    col = lax.broadcasted_iota(jnp.int32, (c, 2 * c), 1) % c
    same = lambda n: (row // n) == (col // n)
    levels = []
    n = 8
    while n < c:
        levels.append(same(2 * n) & jnp.logical_not(same(n)))
        n *= 2
    return row, col, same(8), levels


def _block_diag(y):
    c = y.shape[1]
    lane = lax.broadcasted_iota(jnp.int32, y.shape[1:], 1)[None]
    zero = jnp.zeros((), y.dtype)
    return jnp.concatenate([jnp.where(lane < c, y, zero), jnp.where(lane >= c, y, zero)], axis=1)


def _pair_rows(x0, x1):
    zeros = jnp.zeros_like(x0)
    return jnp.concatenate([jnp.concatenate([x0, zeros], axis=2),
                            jnp.concatenate([zeros, x1], axis=2)], axis=1)


def _dir_where(mask_fwd, mask_bwd, x, other):
    half = x.shape[0] // 2
    return jnp.concatenate([jnp.where(mask_fwd[None], x[:half], other),
                            jnp.where(mask_bwd[None], x[half:], other)], axis=0)


def _unit_tri_inverse_minus_eye(a, base_mask, level_masks):
    bd = lambda y: _block_diag(y.astype(BF16))
    d = jnp.where(base_mask[None], a, 0.0)
    c = a.shape[1]
    x = _bmm(d, bd(d))
    xx_dx = _bmm(jnp.concatenate([x, d], axis=1), bd(x))
    x2 = xx_dx[:, :c]
    m1 = x - d - xx_dx[:, c:]
    n = m1 + x2 + _bmm(m1, bd(x2))
    for mask in level_masks:
        e = jnp.where(mask[None], a, 0.0)
        y = e + _bmm(n, bd(e))
        n = n - y - _bmm(y, bd(n))
    return n


def _lane_broadcast(x, lane):
    lanes = lax.broadcasted_iota(jnp.int32, x.shape, 1)
    col = jnp.sum(jnp.where(lanes == lane, x, 0.0), axis=-1, keepdims=True)
    return jnp.broadcast_to(col, x.shape)


def _delta_kernel(*refs, lay, n_prev):
    if n_prev:
        (qf_ref, kf_ref, vf_ref, gf_ref, qb_ref, kb_ref, vb_ref, gb_ref, s0_ref, prev_ref,
         of_ref, ob_ref, sout_ref, s_ref) = refs
    else:
        (qf_ref, kf_ref, vf_ref, gf_ref, qb_ref, kb_ref, vb_ref, gb_ref, s0_ref,
         of_ref, ob_ref, sout_ref, s_ref) = refs
    i = pl.program_id(0)
    n_ctx_blocks = lay.n_ctx_tok // SEQ_BLOCK
    per_seq = lay.dec_len // SEQ_BLOCK
    nc, nh, c = SEQ_BLOCK // DN_CHUNK, DN_HEADS, DN_CHUNK
    npair = nh // 2
    assert nh % 2 == 0 and 2 * c == LANES and DN_DK == LANES
    heads = [2 * m + parity for parity in range(2) for m in range(npair)]
    is_ctx = i < n_ctx_blocks
    seq_start = is_ctx | ((i - n_ctx_blocks) % per_seq == 0)

    @pl.when(seq_start)
    def _():
        s0 = jnp.stack([s0_ref[d, h] for d in range(2) for h in heads])
        s_ref[...] = jnp.where(is_ctx, 0.0, s0)

    views = ((qf_ref, kf_ref, vf_ref, gf_ref), (qb_ref, kb_ref, vb_ref, gb_ref))
    lane = lax.broadcasted_iota(jnp.int32, (c, LANES), 1)
    qs, ks, vs, gs, bs = [], [], [], [], []
    q2s, k2s, g2s, b2s, gr2s = [], [], [], [], []
    for d, (q_ref, k_ref, v_ref, g_ref) in enumerate(views):
        gates_all = g_ref[...]
        gates_t = jnp.transpose(gates_all)
        for ci in range(nc):
            rows = slice(ci * c, (ci + 1) * c)
            gates = gates_all[rows]
            g_h = [_lane_broadcast(gates, d * nh + h) for h in range(nh)]
            b_h = [_lane_broadcast(gates, (2 + d) * nh + h) for h in range(nh)]
            for h in heads:
                lanes = slice(h * DN_DK, (h + 1) * DN_DK)
                qs.append(q_ref[rows, lanes])
                ks.append(k_ref[rows, lanes])
                vs.append(v_ref[rows, lanes])
                gs.append(g_h[h])
                bs.append(b_h[h])
            for m in range(npair):
                lanes2 = slice(2 * m * DN_DK, (2 * m + 2) * DN_DK)
                q2s.append(q_ref[rows, lanes2])
                k2s.append(k_ref[rows, lanes2])
                g2s.append(jnp.where(lane < c, g_h[2 * m], g_h[2 * m + 1]))
                b2s.append(jnp.where(lane < c, b_h[2 * m], b_h[2 * m + 1]))
                l0 = d * nh + 2 * m
                gr2s.append(jnp.concatenate([gates_t[l0:l0 + 1, rows], gates_t[l0 + 1:l0 + 2, rows]], axis=1))
    q, k, v = jnp.stack(qs), jnp.stack(ks), jnp.stack(vs)
    g, beta = jnp.stack(gs), jnp.stack(bs)
    q2, k2 = jnp.stack(q2s), jnp.stack(k2s)
    g2, beta2, g_row2 = jnp.stack(g2s), jnp.stack(b2s), jnp.stack(gr2s)
    half, half2 = nc * nh, nc * npair
    n_groups = 2 * nc

    row, col, base_mask, level_masks = _chunk_masks()
    lane2 = lax.broadcasted_iota(jnp.int32, (c, 2 * DN_DK), 1)[None]
    k_bd = jnp.concatenate([jnp.where(lane2 < DN_DK, k2, 0.0), jnp.where(lane2 >= DN_DK, k2, 0.0)], axis=1)
    qk_kk = _bmm_nt(jnp.concatenate([q2, k2], axis=1), k_bd)
    decay = jnp.exp(_dir_where(row >= col, row <= col, g2 - g_row2, -jnp.inf))
    a = beta2 * qk_kk[:, c:] * _dir_where(row > col, row < col, decay, 0.0)
    p = qk_kk[:, :c] * decay
    n = _unit_tri_inverse_minus_eye(a, base_mask, level_masks)
    eg = jnp.exp(g)
    g_last = jnp.concatenate([g[:half, c - 1:c], g[half:, 0:1]], axis=0)
    rhs = jnp.concatenate([beta * v, (beta * eg) * k], axis=2)
    width = rhs.shape[2]
    parity_slots = lambda x, par: jnp.concatenate(
        [x[grp * nh + par * npair:grp * nh + (par + 1) * npair] for grp in range(n_groups)], axis=0)
    n_rhs = _bmm(n, _pair_rows(parity_slots(rhs, 0), parity_slots(rhs, 1)))
    sol = rhs + jnp.concatenate(
        [n_rhs[grp * npair:(grp + 1) * npair, :, par * width:(par + 1) * width]
         for grp in range(n_groups) for par in range(2)], axis=0)
    u_t = sol[:, :, :DN_DK]
    wq = jnp.concatenate([sol[:, :, DN_DK:], q * eg], axis=1)
    kd = k * jnp.exp(g_last - g)
    gl = jnp.exp(g_last)

    s = s_ref[...]
    for ci in range(nc):
        cb = nc - 1 - ci
        pick = lambda x: jnp.concatenate([x[ci * nh:(ci + 1) * nh],
                                          x[half + cb * nh:half + (cb + 1) * nh]], axis=0)
        p_now = jnp.concatenate([p[ci * npair:(ci + 1) * npair],
                                 p[half2 + cb * npair:half2 + (cb + 1) * npair]], axis=0)
        ws = _bmm(pick(wq), s)
        u = pick(u_t) - ws[:, :c]
        u_bd = jnp.concatenate([_pair_rows(u[0:npair], u[npair:nh]),
                                _pair_rows(u[nh:nh + npair], u[nh + npair:])], axis=0)
        pu = _bmm(p_now, u_bd)
        o = ws[:, c:] + jnp.concatenate([pu[:npair, :, :DN_DK], pu[:npair, :, DN_DK:],
                                         pu[npair:, :, :DN_DK], pu[npair:, :, DN_DK:]], axis=0)
        s = pick(gl) * s + _bmm_tn(pick(kd), u)
        for slot, h in enumerate(heads):
            lanes = slice(h * DN_DK, (h + 1) * DN_DK)
            of_ref[ci * c:(ci + 1) * c, lanes] = o[slot]
            ob_ref[cb * c:(cb + 1) * c, lanes] = o[nh + slot]
    s_ref[...] = s

    @pl.when(is_ctx)
    def _():
        if n_prev:
            sout_ref[0:n_prev] = prev_ref[...]
        for d in range(2):
            for slot, h in enumerate(heads):
                sout_ref[n_prev, d, h] = s[d * nh + slot]


def _delta_rule(lay, qkv, gates, s0, layer_j, prev_states):
    n_prev = 0 if prev_states is None else prev_states.shape[1]
    assert n_prev == layer_j
    n_blocks = lay.n_tok // SEQ_BLOCK
    n_ctx_blocks = lay.n_ctx_tok // SEQ_BLOCK
    per_seq = lay.dec_len // SEQ_BLOCK
    vd = DN_HEADS * DN_DK

    def rev(i):
        j = i - n_ctx_blocks
        return jnp.where(i < n_ctx_blocks, i, n_ctx_blocks + (j // per_seq) * per_seq + per_seq - 1 - j % per_seq)

    fwd = lambda i: i
    cols = lambda blk, part: pl.BlockSpec((SEQ_BLOCK, vd), lambda i: (blk(i), part))
    gate_spec = lambda blk: pl.BlockSpec((SEQ_BLOCK, LANES), lambda i: (blk(i), 0))
    state_shape = (2, DN_HEADS, DN_DK, DN_DK)
    s0_spec = pl.BlockSpec((None, None) + state_shape,
                           lambda i: (jnp.maximum(i - n_ctx_blocks, 0) // per_seq, layer_j, 0, 0, 0, 0))
    ctx_states = lambda n: pl.BlockSpec((None, n) + state_shape,
                                        lambda i: (jnp.minimum(i, n_ctx_blocks - 1), 0, 0, 0, 0, 0))
    o_shape = jax.ShapeDtypeStruct((lay.n_tok, vd), F32)
    return pl.pallas_call(
        functools.partial(_delta_kernel, lay=lay, n_prev=n_prev),
        out_shape=(o_shape, o_shape,
                   jax.ShapeDtypeStruct((n_ctx_blocks, n_prev + 1) + state_shape, F32)),
        grid=(n_blocks,),
        in_specs=[cols(fwd, 0), cols(fwd, 1), cols(fwd, 2), gate_spec(fwd),
                  cols(rev, 0), cols(rev, 1), cols(rev, 2), gate_spec(rev), s0_spec]
        + ([ctx_states(n_prev)] if n_prev else []),
        out_specs=(cols(fwd, 0), cols(rev, 0), ctx_states(n_prev + 1)),
        scratch_shapes=[pltpu.VMEM((2 * DN_HEADS, DN_DK, DN_DK), F32)],
        compiler_params=_cparams(1),
        name="delta_rule",
    )(qkv, qkv, qkv, gates, qkv, qkv, qkv, gates, s0, *([prev_states] if n_prev else []))


def _dn_out_kernel(x_ref, of_ref, ob_ref, z_ref, mod_ref, ng_ref, w_ref, g_ref, b_ref, o_ref, act_ref,
                   *, alpha):
    for h in range(DN_HEADS):
        lanes = slice(h * DN_DK, (h + 1) * DN_DK)
        o = of_ref[:, lanes] + ob_ref[:, lanes]
        rms = lax.rsqrt(jnp.mean(o * o, axis=-1, keepdims=True) + LN_EPS)
        act_ref[:, lanes] = (o * rms * ng_ref[...] * _silu(z_ref[:, lanes])).astype(BF16)
    y = jnp.dot(act_ref[...], w_ref[...], preferred_element_type=F32)
    o_ref[...] = _residual_norm(x_ref[...], y, 1.0, mod_ref, 1, g_ref, b_ref, alpha)


def _dn_out(lay, x, o_f, o_b, z, mod, layer, norm_g, w_out, g, b, alpha):
    d = lay.d
    vd = DN_HEADS * DN_DK
    return pl.pallas_call(
        functools.partial(_dn_out_kernel, alpha=alpha),
        out_shape=jax.ShapeDtypeStruct((lay.n_tok, d), F32),
        grid=(lay.n_tok // TOK_TILE,),
        in_specs=[_row_spec(TOK_TILE, d), _row_spec(TOK_TILE, vd), _row_spec(TOK_TILE, vd),
                  _row_spec(TOK_TILE, vd), lay.mod_spec(layer, TOK_TILE),
                  _const_spec((1, DN_DK)), _stacked_spec(w_out, layer // N_MIXERS), _const_spec((1, d)),
                  _const_spec((1, d))],
        out_specs=_row_spec(TOK_TILE, d),
        scratch_shapes=[pltpu.VMEM((TOK_TILE, vd), BF16)],
        compiler_params=_cparams(1),
        name="dn_out",
    )(x, o_f, o_b, z, mod, norm_g.reshape(1, DN_DK), w_out, g.reshape(1, d), b.reshape(1, d))


def _dn_mixer(lay, x, mod, layer, w_in, conv_w, a_log, dt_bias, norm_g, w_out, s0, prev_states, g, b, alpha):
    layer_j = layer // N_MIXERS
    qk, vd = DN_HEADS * DN_DK, DN_HEADS * DN_DK
    n_qkv = 2 * qk + vd
    n_gate = 4 * DN_HEADS
    assert w_in.shape[1] == n_qkv + vd + n_gate and n_gate <= LANES
    w_qkvz = w_in[:, :n_qkv + vd].astype(BF16)
    w_ab = jnp.pad(w_in[:, n_qkv + vd:], ((0, 0), (0, LANES - n_gate))).astype(BF16)
    lane_row = lambda v: jnp.pad(v.reshape(1, 2 * DN_HEADS), ((0, 0), (0, LANES - 2 * DN_HEADS)))
    assert conv_w.shape[0] == 3
    conv_pad = jnp.pad(conv_w, ((0, SUBLANES - conv_w.shape[0]), (0, 0)))
    qkv, z, gates = _dn_in(lay, x, mod, layer, w_qkvz, w_ab, conv_pad, lane_row(a_log), lane_row(dt_bias), n_qkv)
    o_f, o_b, s_fin = _delta_rule(lay, qkv, gates, s0, layer_j, prev_states)
    x = _dn_out(lay, x, o_f, o_b, z, mod, layer, norm_g, w_out, g, b, alpha)
    return x, s_fin


def kernel(x_prompt, x_sample, state_delta, c, c_ctx, w_mod, b_mod, ln_g, ln_b, ffn_w_in, ffn_w_out,
           cv_w1, cv_b1, cv_dw, cv_dwb, cv_ln_g, cv_ln_b, cv_w2, cv_b2,
           dn_w_in, dn_conv, dn_a_log, dn_dt_bias, dn_norm_g, dn_w_out):
    n_ctx_seq, ctx_len, d = x_prompt.shape
    n_dec_seq, dec_len, _ = x_sample.shape
    depth = w_mod.shape[0]
    alpha = (2.0 * depth) ** 0.25
    lay = _Layout(n_ctx_seq, ctx_len, n_dec_seq, dec_len, d)

    x = (x_prompt.reshape(lay.n_ctx_tok, d), x_sample.reshape(lay.n_dec_tok, d))
    cond = jnp.concatenate([c_ctx[None, :], c, jnp.zeros((COND_ROWS - 1 - n_dec_seq, d), F32)], axis=0)
    mod = _adaln(cond, w_mod, b_mod).reshape(depth, COND_ROWS, N_MOD, d)

    ffn_w_in, ffn_w_out = ffn_w_in.astype(BF16), ffn_w_out.astype(BF16)
    cv_w1, cv_w2, dn_w_out = cv_w1.astype(BF16), cv_w2.astype(BF16), dn_w_out.astype(BF16)
    ffn_g, ffn_b = ln_g.reshape(depth, 3, 1, d), ln_b.reshape(depth, 3, 1, d)

    states = None
    for l in range(depth):
        x = _ffn(lay, x, mod, l, 0, ffn_w_in, ffn_w_out, ffn_g, ffn_b, alpha)
        j = l // N_MIXERS
        if l % N_MIXERS == 0:
            x = _conv_mixer(lay, x, mod, l, cv_w1, cv_b1[j], cv_dw[j], cv_dwb[j],
                            cv_ln_g[j], cv_ln_b[j], cv_w2, cv_b2[j], ln_g[l, 1], ln_b[l, 1], alpha)
        else:
            x, states = _dn_mixer(lay, x, mod, l, dn_w_in[j], dn_conv[j], dn_a_log[j], dn_dt_bias[j],
                                  dn_norm_g[j], dn_w_out, state_delta, states, ln_g[l, 1], ln_b[l, 1], alpha)
        x = _ffn(lay, x, mod, l, 1, ffn_w_in, ffn_w_out, ffn_g, ffn_b, alpha, split_out=(l == depth - 1))

    y_prompt, y_sample = x
    return (y_prompt.reshape(n_ctx_seq, ctx_len, d), y_sample.reshape(n_dec_seq, dec_len, d),
            states.astype(state_delta.dtype))
```

```python
import functools

import jax
import jax.numpy as jnp
from jax import lax
from jax.experimental import pallas as pl
from jax.experimental.pallas import tpu as pltpu

F32 = jnp.float32
BF16 = jnp.bfloat16

LN_EPS = 1e-5
L2_EPS = 1e-6
N_MOD = 9
N_MIXERS = 2
GRID_W = 64
DN_HEADS = 8
DN_DK = 128
DN_CHUNK = 64
COND_ROWS = 8
LANES = 128
VMEM_LIMIT = 56 * 1024 * 1024

TOK_TILE = 512
FFN_TILE = 1024
SEQ_BLOCK = 256
FF_CHUNK = 256


def _cparams(n_axes):
    return pltpu.CompilerParams(dimension_semantics=("arbitrary",) * n_axes,
                                vmem_limit_bytes=VMEM_LIMIT)


def _mm(a, b):
    return jnp.dot(a.astype(BF16), b.astype(BF16), preferred_element_type=F32)


def _mm_nt(a, b):
    return lax.dot_general(a.astype(BF16), b.astype(BF16), (((1,), (1,)), ((), ())),
                           preferred_element_type=F32)


def _mm_tn(a, b):
    return lax.dot_general(a.astype(BF16), b.astype(BF16), (((0,), (0,)), ((), ())),
                           preferred_element_type=F32)


def _sigmoid(x):
    return 1.0 / (1.0 + jnp.exp(-x))


def _silu(x):
    return x * _sigmoid(x)


def _layer_norm(r, g, b):
    mu = jnp.mean(r, axis=-1, keepdims=True)
    rc = r - mu
    var = jnp.mean(rc * rc, axis=-1, keepdims=True)
    return rc * lax.rsqrt(var + LN_EPS) * g + b


def _modulate(x, mod_ref, slot):
    shift = mod_ref[3 * slot:3 * slot + 1, :]
    scale = mod_ref[3 * slot + 1:3 * slot + 2, :]
    return x * (1.0 + scale) + shift


def _residual_norm(x, y, gate_scale, mod_ref, slot, g_ref, b_ref, alpha):
    gate = mod_ref[3 * slot + 2:3 * slot + 3, :]
    return _layer_norm(alpha * x + (gate_scale * gate) * y, g_ref[...], b_ref[...])


class _Layout:
    def __init__(self, n_ctx_seq, ctx_len, n_dec_seq, dec_len, d_model):
        self.n_ctx_seq, self.ctx_len = n_ctx_seq, ctx_len
        self.n_dec_seq, self.dec_len = n_dec_seq, dec_len
        self.n_ctx_tok = n_ctx_seq * ctx_len
        self.n_dec_tok = n_dec_seq * dec_len
        self.n_tok = self.n_ctx_tok + self.n_dec_tok
        self.d = d_model
        assert n_dec_seq + 1 <= COND_ROWS
        assert self.n_ctx_tok % TOK_TILE == 0 and dec_len % TOK_TILE == 0
        assert ctx_len == SEQ_BLOCK and dec_len % SEQ_BLOCK == 0
        assert dec_len % GRID_W == 0 and SEQ_BLOCK % GRID_W == 0

    def mod_row(self, i, tile):
        n_ctx_tiles = self.n_ctx_tok // tile
        per_seq = self.dec_len // tile
        return jnp.where(i < n_ctx_tiles, 0, 1 + (i - n_ctx_tiles) // per_seq)

    def mod_spec(self, layer, tile):
        return pl.BlockSpec((None, None, N_MOD, self.d),
                            lambda i: (layer, self.mod_row(i, tile), 0, 0))

    def group_specs(self, tile, width):
        n_ctx_tiles = self.n_ctx_tok // tile
        return (pl.BlockSpec((tile, width), lambda i: (jnp.minimum(i, n_ctx_tiles - 1), 0)),
                pl.BlockSpec((tile, width), lambda i: (jnp.maximum(i - n_ctx_tiles, 0), 0)))


def _row_spec(tile, width, col=0):
    return pl.BlockSpec((tile, width), lambda i: (i, col))


def _const_spec(shape):
    return pl.BlockSpec(shape, lambda *_: (0,) * len(shape))


def _stacked_spec(stacked, index):
    shape = stacked.shape[1:]
    return pl.BlockSpec((None,) + shape, lambda *_: (index,) + (0,) * len(shape))


def _adaln_kernel(cond_ref, w_ref, b_ref, o_ref):
    cnd = cond_ref[...]
    o_ref[...] = jnp.dot(_silu(cnd), w_ref[...], preferred_element_type=F32,
                         precision=lax.Precision.HIGHEST) + b_ref[...]


def _adaln(cond, w_mod, b_mod):
    depth, d, n_out = w_mod.shape
    bn = n_out // 4
    return pl.pallas_call(
        _adaln_kernel,
        out_shape=jax.ShapeDtypeStruct((depth, COND_ROWS, n_out), F32),
        grid=(depth, n_out // bn),
        in_specs=[pl.BlockSpec((COND_ROWS, d), lambda l, j: (0, 0)),
                  pl.BlockSpec((None, d, bn), lambda l, j: (l, 0, j)),
                  pl.BlockSpec((None, 1, bn), lambda l, j: (l, 0, j))],
        out_specs=pl.BlockSpec((None, COND_ROWS, bn), lambda l, j: (l, 0, j)),
        compiler_params=_cparams(2),
        name="adaln",
    )(cond, w_mod, b_mod.reshape(depth, 1, n_out))


def _ffn_kernel(*refs, slot, d_ff, alpha, n_ctx_tiles, split_in, split_out):
    n_in = 2 if split_in else 1
    n_out = 2 if split_out else 1
    x_refs, (mod_ref, w_in_ref, w_out_ref, g_ref, b_ref) = refs[:n_in], refs[n_in:n_in + 5]
    o_refs, act_ref = refs[n_in + 5:n_in + 5 + n_out], refs[-1]
    is_ctx = pl.program_id(0) < n_ctx_tiles
    x = jnp.where(is_ctx, x_refs[0][...], x_refs[1][...]) if split_in else x_refs[0][...]
    h = _modulate(x, mod_ref, slot).astype(BF16)
    for j in range(d_ff // FF_CHUNK):
        lo = j * FF_CHUNK
        lin = jnp.dot(h, w_in_ref[:, lo:lo + FF_CHUNK], preferred_element_type=F32)
        gat = jnp.dot(h, w_in_ref[:, d_ff + lo:d_ff + lo + FF_CHUNK], preferred_element_type=F32)
        act_ref[:, lo:lo + FF_CHUNK] = (_silu(gat) * lin).astype(BF16)
    y = jnp.dot(act_ref[...], w_out_ref[...], preferred_element_type=F32)
    out = _residual_norm(x, y, 0.5, mod_ref, slot, g_ref, b_ref, alpha)
    if not split_out:
        o_refs[0][...] = out
    else:
        @pl.when(is_ctx)
        def _():
            o_refs[0][...] = out

        @pl.when(jnp.logical_not(is_ctx))
        def _():
            o_refs[1][...] = out


def _ffn(lay, xs, mod, layer, sub, w_in, w_out, g, b, alpha, split_out=False):
    d, d_ff = lay.d, w_out.shape[2]
    assert d_ff % FF_CHUNK == 0
    split_in = isinstance(xs, tuple)
    tile = FFN_TILE
    assert lay.n_ctx_tok % tile == 0 and lay.dec_len % tile == 0
    x_specs = list(lay.group_specs(tile, d)) if split_in else [_row_spec(tile, d)]
    if split_out:
        out_shape = (jax.ShapeDtypeStruct((lay.n_ctx_tok, d), F32), jax.ShapeDtypeStruct((lay.n_dec_tok, d), F32))
        out_specs = lay.group_specs(tile, d)
    else:
        out_shape = jax.ShapeDtypeStruct((lay.n_tok, d), F32)
        out_specs = _row_spec(tile, d)
    once = pl.Buffered(1)
    return pl.pallas_call(
        functools.partial(_ffn_kernel, slot=2 * sub, d_ff=d_ff, alpha=alpha,
                          n_ctx_tiles=lay.n_ctx_tok // tile, split_in=split_in, split_out=split_out),
        out_shape=out_shape,
        grid=(lay.n_tok // tile,),
        in_specs=x_specs + [lay.mod_spec(layer, tile),
                            pl.BlockSpec((None, None, d, 2 * d_ff), lambda i: (layer, sub, 0, 0),
                                         pipeline_mode=once),
                            pl.BlockSpec((None, None, d_ff, d), lambda i: (layer, sub, 0, 0),
                                         pipeline_mode=once),
                            pl.BlockSpec((None, None, 1, d), lambda i: (layer, 2 * sub, 0, 0)),
                            pl.BlockSpec((None, None, 1, d), lambda i: (layer, 2 * sub, 0, 0))],
        out_specs=out_specs,
        scratch_shapes=[pltpu.VMEM((tile, d_ff), BF16)],
        compiler_params=_cparams(1),
        name="ffn",
    )(*(xs if split_in else (xs,)), mod, w_in, w_out, g, b)


def _conv_in_kernel(x_ref, mod_ref, w1_ref, b1_ref, u_ref, *, d):
    h = _modulate(x_ref[...], mod_ref, 1).astype(BF16)
    lin = jnp.dot(h, w1_ref[:, :d], preferred_element_type=F32) + b1_ref[:, :d]
    gat = jnp.dot(h, w1_ref[:, d:], preferred_element_type=F32) + b1_ref[:, d:]
    u_ref[...] = lin * _sigmoid(gat)


def _conv_in(lay, x, mod, layer, w1, b1):
    d = lay.d
    return pl.pallas_call(
        functools.partial(_conv_in_kernel, d=d),
        out_shape=jax.ShapeDtypeStruct((lay.n_tok, d), F32),
        grid=(lay.n_tok // TOK_TILE,),
        in_specs=[_row_spec(TOK_TILE, d), lay.mod_spec(layer, TOK_TILE),
                  _stacked_spec(w1, layer // N_MIXERS), _const_spec((1, 2 * d))],
        out_specs=_row_spec(TOK_TILE, d),
        compiler_params=_cparams(1),
        name="conv_in",
    )(x, mod, w1, b1.reshape(1, 2 * d))


CONV_PAD = 16
CONV_ROWS = 64


SUBLANES = 8
DEC_CONV_CH = 256


def _segment_conv(u_ref, w_ref, o_ref, pad_ref, sh_ref, *, taps, seg_len, n_seg):
    n_ch = u_ref.shape[1]
    half = taps // 2
    n_sh = sh_ref.shape[1]
    zeros = jnp.zeros((CONV_PAD, n_ch), F32)
    pad_ref[0:CONV_PAD, :] = zeros
    pad_ref[CONV_PAD + seg_len:2 * CONV_PAD + seg_len, :] = zeros

    def segment(s, carry):
        row0 = pl.multiple_of(s * seg_len, seg_len)
        pad_ref[CONV_PAD:CONV_PAD + seg_len, :] = u_ref[pl.ds(row0, seg_len), :]

        def lane_tile(j, carry):
            lanes = pl.ds(pl.multiple_of(j * LANES, LANES), LANES)
            for p in range(1, SUBLANES):
                sh_ref[p] = pad_ref[p:p + n_sh, lanes]
            for r in range(seg_len // CONV_ROWS):
                acc = jnp.zeros((CONV_ROWS, LANES), F32)
                for k in range(taps):
                    off = CONV_PAD - half + k
                    p = off % SUBLANES
                    start = off - p + r * CONV_ROWS
                    src = (pad_ref[start:start + CONV_ROWS, lanes] if p == 0
                           else sh_ref[p, start:start + CONV_ROWS, :])
                    acc = acc + w_ref[k:k + 1, lanes] * src
                o_ref[pl.ds(row0 + r * CONV_ROWS, CONV_ROWS), lanes] = acc
            return carry

        return lax.fori_loop(0, n_ch // LANES, lane_tile, carry)

    lax.fori_loop(0, n_seg, segment, 0)


def _segment_conv_scratch(seg_len, n_ch):
    return [pltpu.VMEM((seg_len + 2 * CONV_PAD, n_ch), F32),
            pltpu.VMEM((SUBLANES, seg_len + 2 * CONV_PAD - SUBLANES, LANES), F32)]


def _ctx_conv_kernel(u_ref, w_ref, o_ref, pad_ref, sh_ref, *, taps):
    _segment_conv(u_ref, w_ref, o_ref, pad_ref, sh_ref, taps=taps, seg_len=u_ref.shape[0], n_seg=1)


def _ctx_conv(lay, u, w, taps):
    d = lay.d
    return pl.pallas_call(
        functools.partial(_ctx_conv_kernel, taps=taps),
        out_shape=jax.ShapeDtypeStruct((lay.n_ctx_tok, d), F32),
        grid=(lay.n_ctx_seq,),
        in_specs=[pl.BlockSpec((lay.ctx_len, d), lambda i: (i, 0)),
                  pl.BlockSpec((w.shape[0], d), lambda i: (0, 0))],
        out_specs=pl.BlockSpec((lay.ctx_len, d), lambda i: (i, 0)),
        scratch_shapes=_segment_conv_scratch(lay.ctx_len, d),
        compiler_params=_cparams(1),
        name="ctx_conv",
    )(u, w)


def _dec_conv_kernel(u_ref, w_ref, o_ref, pad_ref, sh_ref, vpad_ref, *, taps, n_row_blocks):
    j = pl.program_id(1)
    seq_len, n_ch = u_ref.shape

    @pl.when(j < n_row_blocks)
    def _():
        _segment_conv(u_ref, w_ref, o_ref, pad_ref, sh_ref, taps=taps, seg_len=GRID_W,
                      n_seg=seq_len // GRID_W)

    @pl.when(j >= n_row_blocks)
    def _():
        halo = (taps // 2) * GRID_W
        zeros = jnp.zeros((halo, n_ch), F32)
        vpad_ref[0:halo, :] = zeros
        vpad_ref[halo + seq_len:2 * halo + seq_len, :] = zeros
        vpad_ref[halo:halo + seq_len, :] = u_ref[...]

        def row_chunk(r, carry):
            base = pl.multiple_of(r * CONV_ROWS, CONV_ROWS)
            for t in range(n_ch // LANES):
                lanes = slice(t * LANES, (t + 1) * LANES)
                acc = jnp.zeros((CONV_ROWS, LANES), F32)
                for k in range(taps):
                    acc = acc + w_ref[k:k + 1, lanes] * vpad_ref[pl.ds(base + k * GRID_W, CONV_ROWS), lanes]
                o_ref[pl.ds(base, CONV_ROWS), lanes] = acc
            return carry

        lax.fori_loop(0, seq_len // CONV_ROWS, row_chunk, 0)


def _dec_conv(lay, u, w, taps):
    d = lay.d
    first_seq = lay.n_ctx_tok // lay.dec_len
    assert lay.n_ctx_tok % lay.dec_len == 0 and (d // 2) % DEC_CONV_CH == 0
    halo = (taps // 2) * GRID_W
    return pl.pallas_call(
        functools.partial(_dec_conv_kernel, taps=taps, n_row_blocks=(d // 2) // DEC_CONV_CH),
        out_shape=jax.ShapeDtypeStruct((lay.n_dec_tok, d), F32),
        grid=(lay.n_dec_seq, d // DEC_CONV_CH),
        in_specs=[pl.BlockSpec((lay.dec_len, DEC_CONV_CH), lambda b, j: (first_seq + b, j)),
                  pl.BlockSpec((w.shape[0], DEC_CONV_CH), lambda b, j: (0, j))],
        out_specs=pl.BlockSpec((lay.dec_len, DEC_CONV_CH), lambda b, j: (b, j)),
        scratch_shapes=_segment_conv_scratch(GRID_W, DEC_CONV_CH)
        + [pltpu.VMEM((lay.dec_len + 2 * halo, DEC_CONV_CH), F32)],
        compiler_params=_cparams(2),
        name="dec_conv",
    )(u, w)


def _conv_out_kernel(x_ref, cvc_ref, cvd_ref, mod_ref, dwb_ref, lng_ref, lnb_ref, w2_ref, b2_ref, g_ref, b_ref,
                     o_ref, *, alpha, n_ctx_tiles):
    cv = jnp.where(pl.program_id(0) < n_ctx_tiles, cvc_ref[...], cvd_ref[...])
    a = _silu(_layer_norm(cv + dwb_ref[...], lng_ref[...], lnb_ref[...]))
    y = jnp.dot(a.astype(BF16), w2_ref[...], preferred_element_type=F32) + b2_ref[...]
    o_ref[...] = _residual_norm(x_ref[...], y, 1.0, mod_ref, 1, g_ref, b_ref, alpha)


def _conv_out(lay, x, cv_ctx, cv_dec, mod, layer, dwb, lng, lnb, w2, b2, g, b, alpha):
    d = lay.d
    vec = lambda v: v.reshape(1, d)
    return pl.pallas_call(
        functools.partial(_conv_out_kernel, alpha=alpha, n_ctx_tiles=lay.n_ctx_tok // TOK_TILE),
        out_shape=jax.ShapeDtypeStruct((lay.n_tok, d), F32),
        grid=(lay.n_tok // TOK_TILE,),
        in_specs=[_row_spec(TOK_TILE, d), *lay.group_specs(TOK_TILE, d), lay.mod_spec(layer, TOK_TILE),
                  _const_spec((1, d)), _const_spec((1, d)), _const_spec((1, d)),
                  _stacked_spec(w2, layer // N_MIXERS), _const_spec((1, d)), _const_spec((1, d)),
                  _const_spec((1, d))],
        out_specs=_row_spec(TOK_TILE, d),
        compiler_params=_cparams(1),
        name="conv_out",
    )(x, cv_ctx, cv_dec, mod, vec(dwb), vec(lng), vec(lnb), w2, vec(b2), vec(g), vec(b))


def _conv_mixer(lay, x, mod, layer, w1, b1, dw, dwb, lng, lnb, w2, b2, g, b, alpha):
    taps = dw.shape[0]
    assert taps // 2 < CONV_PAD
    u = _conv_in(lay, x, mod, layer, w1, b1)
    dw_pad = jnp.pad(dw, ((0, -taps % SUBLANES), (0, 0)))
    cv_ctx = _ctx_conv(lay, u, dw_pad, taps)
    cv_dec = _dec_conv(lay, u, dw_pad, taps)
    return _conv_out(lay, x, cv_ctx, cv_dec, mod, layer, dwb, lng, lnb, w2, b2, g, b, alpha)


def _split3(x):
    hi = x.astype(BF16)
    r1 = x - hi.astype(F32)
    mid = r1.astype(BF16)
    lo = (r1 - mid.astype(F32)).astype(BF16)
    return hi, mid, lo


def _chunk_cumsums(tri, x):
    n = x.shape[1]
    parts = jnp.dot(tri, jnp.concatenate(_split3(x), axis=1), preferred_element_type=F32)
    sums = parts[:, :n] + parts[:, n:2 * n] + parts[:, 2 * n:]
    return sums[:x.shape[0]], sums[x.shape[0]:]


DN_HALO = 16
DN_COLS = 256


def _dn_in_kernel(x_ref, xp_ref, xn_ref, mod_ref, w_ref, wab_ref, cw_ref, alog_ref, dtb_ref,
                  qkv_ref, gates_ref, h_ref, tri_ref, *, lay, n_qkv):
    i = pl.program_id(0)
    n_ctx_blocks = lay.n_ctx_tok // SEQ_BLOCK
    per_seq = lay.dec_len // SEQ_BLOCK
    pos = (i - n_ctx_blocks) % per_seq
    is_ctx = i < n_ctx_blocks
    has_prev = jnp.logical_not(is_ctx | (pos == 0))
    has_next = jnp.logical_not(is_ctx | (pos == per_seq - 1))
    lo, hi = DN_HALO, DN_HALO + SEQ_BLOCK
    h_ref[0:lo] = jnp.where(has_prev, _modulate(xp_ref[...], mod_ref, 1), 0.0).astype(BF16)
    h_ref[lo:hi] = _modulate(x_ref[...], mod_ref, 1).astype(BF16)
    h_ref[hi:hi + DN_HALO] = jnp.where(has_next, _modulate(xn_ref[...], mod_ref, 1), 0.0).astype(BF16)

    n_qk = 2 * DN_HEADS * DN_DK
    n_rows = h_ref.shape[0]
    for j in range(n_qkv // DN_COLS):
        cols = slice(j * DN_COLS, (j + 1) * DN_COLS)
        pre = jnp.dot(h_ref[...], w_ref[:, cols], preferred_element_type=F32)
        y = _silu(cw_ref[0:1, cols] * pltpu.roll(pre, 1, 0)[lo:hi] + cw_ref[1:2, cols] * pre[lo:hi]
                  + cw_ref[2:3, cols] * pltpu.roll(pre, n_rows - 1, 0)[lo:hi])
        for t in range(DN_COLS // DN_DK):
            c0 = j * DN_COLS + t * DN_DK
            yt = y[:, t * DN_DK:(t + 1) * DN_DK]
            if c0 < n_qk:
                inv = lax.rsqrt(jnp.sum(yt * yt, axis=-1, keepdims=True) + L2_EPS)
                yt = yt * (inv * (DN_DK ** -0.5 if c0 < n_qk // 2 else 1.0))
            qkv_ref[:, c0:c0 + DN_DK] = yt

    ab = jnp.dot(h_ref[lo:hi], wab_ref[...], preferred_element_type=F32)
    pre = ab + dtb_ref[...]
    softplus = jnp.maximum(pre, 0.0) + jnp.log(1.0 + jnp.exp(-jnp.abs(pre)))
    log_a = -jnp.exp(alog_ref[...]) * softplus
    beta = _sigmoid(ab)

    @pl.when(i == 0)
    def _():
        tile = ab.shape[0]
        row = lax.broadcasted_iota(jnp.int32, (tile, tile), 0)
        col = lax.broadcasted_iota(jnp.int32, (tile, tile), 1)
        same = (row // DN_CHUNK) == (col // DN_CHUNK)
        tri_ref[0:tile] = (same & (row >= col)).astype(BF16)
        tri_ref[tile:2 * tile] = (same & (row <= col)).astype(BF16)

    lane = lax.broadcasted_iota(jnp.int32, ab.shape, 1)
    g_fwd, g_bwd = _chunk_cumsums(tri_ref[...], log_a)
    gates_ref[...] = jnp.where(lane < DN_HEADS, g_fwd, jnp.where(lane < 2 * DN_HEADS, g_bwd, beta))


def _dn_in(lay, x, mod, layer, w_qkvz, w_ab, conv_w, alog_row, dtb_row, n_qkv):
    d = w_qkvz.shape[0]
    assert n_qkv % DN_COLS == 0 and SEQ_BLOCK % DN_HALO == 0
    sub = SEQ_BLOCK // DN_HALO
    n_units = lay.n_tok // DN_HALO
    return pl.pallas_call(
        functools.partial(_dn_in_kernel, lay=lay, n_qkv=n_qkv),
        out_shape=(jax.ShapeDtypeStruct((lay.n_tok, n_qkv), F32),
                   jax.ShapeDtypeStruct((lay.n_tok, LANES), F32)),
        grid=(lay.n_tok // SEQ_BLOCK,),
        in_specs=[_row_spec(SEQ_BLOCK, d),
                  pl.BlockSpec((DN_HALO, d), lambda i: (jnp.maximum(i * sub - 1, 0), 0)),
                  pl.BlockSpec((DN_HALO, d), lambda i: (jnp.minimum((i + 1) * sub, n_units - 1), 0)),
                  lay.mod_spec(layer, SEQ_BLOCK),
                  _const_spec((d, n_qkv)), _const_spec((d, LANES)), _const_spec((SUBLANES, n_qkv)),
                  _const_spec((1, LANES)), _const_spec((1, LANES))],
        out_specs=(_row_spec(SEQ_BLOCK, n_qkv), _row_spec(SEQ_BLOCK, LANES)),
        scratch_shapes=[pltpu.VMEM((SEQ_BLOCK + 2 * DN_HALO, d), BF16),
                        pltpu.VMEM((2 * SEQ_BLOCK, SEQ_BLOCK), BF16)],
        compiler_params=_cparams(1),
        name="dn_in",
    )(x, x, x, mod, w_qkvz, w_ab, conv_w, alog_row, dtb_row)


def _bmm(a, b):
    return jnp.einsum("bmk,bkn->bmn", a.astype(BF16), b.astype(BF16), preferred_element_type=F32)


def _bmm_nt(a, b):
    return jnp.einsum("bmk,bnk->bmn", a.astype(BF16), b.astype(BF16), preferred_element_type=F32)


def _bmm_tn(a, b):
    return jnp.einsum("bkm,bkn->bmn", a.astype(BF16), b.astype(BF16), preferred_element_type=F32)


def _chunk_masks():
    c = DN_CHUNK
    row = lax.broadcasted_iota(jnp.int32, (c, 2 * c), 0)
    col = lax.broadcasted_iota(jnp.int32, (c, 2 * c), 1) % c
    same = lambda n: (row // n) == (col // n)
    levels = []
    n = 8
    while n < c:
        levels.append(same(2 * n) & jnp.logical_not(same(n)))
        n *= 2
    return row, col, same(8), levels


def _block_diag(y):
    c = y.shape[1]
    lane = lax.broadcasted_iota(jnp.int32, y.shape[1:], 1)[None]
    zero = jnp.zeros((), y.dtype)
    return jnp.concatenate([jnp.where(lane < c, y, zero), jnp.where(lane >= c, y, zero)], axis=1)


def _pair_rows(x0, x1):
    zeros = jnp.zeros_like(x0)
    return jnp.concatenate([jnp.concatenate([x0, zeros], axis=2),
                            jnp.concatenate([zeros, x1], axis=2)], axis=1)


def _dir_where(mask_fwd, mask_bwd, x, other):
    half = x.shape[0] // 2
    return jnp.concatenate([jnp.where(mask_fwd[None], x[:half], other),
                            jnp.where(mask_bwd[None], x[half:], other)], axis=0)


def _unit_tri_inverse_minus_eye(a, base_mask, level_masks):
    bd = lambda y: _block_diag(y.astype(BF16))
    d = jnp.where(base_mask[None], a, 0.0)
    c = a.shape[1]
    x = _bmm(d, bd(d))
    xx_dx = _bmm(jnp.concatenate([x, d], axis=1), bd(x))
    x2 = xx_dx[:, :c]
    m1 = x - d - xx_dx[:, c:]
    n = m1 + x2 + _bmm(m1, bd(x2))
    for mask in level_masks:
        e = jnp.where(mask[None], a, 0.0)
        y = e + _bmm(n, bd(e))
        n = n - y - _bmm(y, bd(n))
    return n


def _lane_broadcast(x, lane):
    lanes = lax.broadcasted_iota(jnp.int32, x.shape, 1)
    col = jnp.sum(jnp.where(lanes == lane, x, 0.0), axis=-1, keepdims=True)
    return jnp.broadcast_to(col, x.shape)


def _delta_kernel(*refs, lay, n_prev):
    if n_prev:
        (qf_ref, kf_ref, vf_ref, gf_ref, qb_ref, kb_ref, vb_ref, gb_ref, s0_ref, prev_ref,
         of_ref, ob_ref, sout_ref, s_ref) = refs
    else:
        (qf_ref, kf_ref, vf_ref, gf_ref, qb_ref, kb_ref, vb_ref, gb_ref, s0_ref,
         of_ref, ob_ref, sout_ref, s_ref) = refs
    i = pl.program_id(0)
    n_ctx_blocks = lay.n_ctx_tok // SEQ_BLOCK
    per_seq = lay.dec_len // SEQ_BLOCK
    nc, nh, c = SEQ_BLOCK // DN_CHUNK, DN_HEADS, DN_CHUNK
    npair = nh // 2
    assert nh % 2 == 0 and 2 * c == LANES and DN_DK == LANES
    heads = [2 * m + parity for parity in range(2) for m in range(npair)]
    is_ctx = i < n_ctx_blocks
    seq_start = is_ctx | ((i - n_ctx_blocks) % per_seq == 0)

    @pl.when(seq_start)
    def _():
        s0 = jnp.stack([s0_ref[d, h] for d in range(2) for h in heads])
        s_ref[...] = jnp.where(is_ctx, 0.0, s0)

    views = ((qf_ref, kf_ref, vf_ref, gf_ref), (qb_ref, kb_ref, vb_ref, gb_ref))
    lane = lax.broadcasted_iota(jnp.int32, (c, LANES), 1)
    qs, ks, vs, gs, bs = [], [], [], [], []
    q2s, k2s, g2s, b2s, gr2s = [], [], [], [], []
    for d, (q_ref, k_ref, v_ref, g_ref) in enumerate(views):
        gates_all = g_ref[...]
        gates_t = jnp.transpose(gates_all)
        for ci in range(nc):
            rows = slice(ci * c, (ci + 1) * c)
            gates = gates_all[rows]
            g_h = [_lane_broadcast(gates, d * nh + h) for h in range(nh)]
            b_h = [_lane_broadcast(gates, (2 + d) * nh + h) for h in range(nh)]
            for h in heads:
                lanes = slice(h * DN_DK, (h + 1) * DN_DK)
                qs.append(q_ref[rows, lanes])
                ks.append(k_ref[rows, lanes])
                vs.append(v_ref[rows, lanes])
                gs.append(g_h[h])
                bs.append(b_h[h])
            for m in range(npair):
                lanes2 = slice(2 * m * DN_DK, (2 * m + 2) * DN_DK)
                q2s.append(q_ref[rows, lanes2])
                k2s.append(k_ref[rows, lanes2])
                g2s.append(jnp.where(lane < c, g_h[2 * m], g_h[2 * m + 1]))
                b2s.append(jnp.where(lane < c, b_h[2 * m], b_h[2 * m + 1]))
                l0 = d * nh + 2 * m
                gr2s.append(jnp.concatenate([gates_t[l0:l0 + 1, rows], gates_t[l0 + 1:l0 + 2, rows]], axis=1))
    q, k, v = jnp.stack(qs), jnp.stack(ks), jnp.stack(vs)
    g, beta = jnp.stack(gs), jnp.stack(bs)
    q2, k2 = jnp.stack(q2s), jnp.stack(k2s)
    g2, beta2, g_row2 = jnp.stack(g2s), jnp.stack(b2s), jnp.stack(gr2s)
    half, half2 = nc * nh, nc * npair
    n_groups = 2 * nc

    row, col, base_mask, level_masks = _chunk_masks()
    lane2 = lax.broadcasted_iota(jnp.int32, (c, 2 * DN_DK), 1)[None]
    k_bd = jnp.concatenate([jnp.where(lane2 < DN_DK, k2, 0.0), jnp.where(lane2 >= DN_DK, k2, 0.0)], axis=1)
    qk_kk = _bmm_nt(jnp.concatenate([q2, k2], axis=1), k_bd)
    decay = jnp.exp(_dir_where(row >= col, row <= col, g2 - g_row2, -jnp.inf))
    a = beta2 * qk_kk[:, c:] * _dir_where(row > col, row < col, decay, 0.0)
    p = qk_kk[:, :c] * decay
    n = _unit_tri_inverse_minus_eye(a, base_mask, level_masks)
    eg = jnp.exp(g)
    g_last = jnp.concatenate([g[:half, c - 1:c], g[half:, 0:1]], axis=0)
    rhs = jnp.concatenate([beta * v, (beta * eg) * k], axis=2)
    width = rhs.shape[2]
    parity_slots = lambda x, par: jnp.concatenate(
        [x[grp * nh + par * npair:grp * nh + (par + 1) * npair] for grp in range(n_groups)], axis=0)
    n_rhs = _bmm(n, _pair_rows(parity_slots(rhs, 0), parity_slots(rhs, 1)))
    sol = rhs + jnp.concatenate(
        [n_rhs[grp * npair:(grp + 1) * npair, :, par * width:(par + 1) * width]
         for grp in range(n_groups) for par in range(2)], axis=0)
    u_t = sol[:, :, :DN_DK]
    wq = jnp.concatenate([sol[:, :, DN_DK:], q * eg], axis=1)
    kd = k * jnp.exp(g_last - g)
    gl = jnp.exp(g_last)

    s = s_ref[...]
    for ci in range(nc):
        cb = nc - 1 - ci
        pick = lambda x: jnp.concatenate([x[ci * nh:(ci + 1) * nh],
                                          x[half + cb * nh:half + (cb + 1) * nh]], axis=0)
        p_now = jnp.concatenate([p[ci * npair:(ci + 1) * npair],
                                 p[half2 + cb * npair:half2 + (cb + 1) * npair]], axis=0)
        ws = _bmm(pick(wq), s)
        u = pick(u_t) - ws[:, :c]
        u_bd = jnp.concatenate([_pair_rows(u[0:npair], u[npair:nh]),
                                _pair_rows(u[nh:nh + npair], u[nh + npair:])], axis=0)
        pu = _bmm(p_now, u_bd)
        o = ws[:, c:] + jnp.concatenate([pu[:npair, :, :DN_DK], pu[:npair, :, DN_DK:],
                                         pu[npair:, :, :DN_DK], pu[npair:, :, DN_DK:]], axis=0)
        s = pick(gl) * s + _bmm_tn(pick(kd), u)
        for slot, h in enumerate(heads):
            lanes = slice(h * DN_DK, (h + 1) * DN_DK)
            of_ref[ci * c:(ci + 1) * c, lanes] = o[slot]
            ob_ref[cb * c:(cb + 1) * c, lanes] = o[nh + slot]
    s_ref[...] = s

    @pl.when(is_ctx)
    def _():
        if n_prev:
            sout_ref[0:n_prev] = prev_ref[...]
        for d in range(2):
            for slot, h in enumerate(heads):
                sout_ref[n_prev, d, h] = s[d * nh + slot]


def _delta_rule(lay, qkv, gates, s0, layer_j, prev_states):
    n_prev = 0 if prev_states is None else prev_states.shape[1]
    assert n_prev == layer_j
    n_blocks = lay.n_tok // SEQ_BLOCK
    n_ctx_blocks = lay.n_ctx_tok // SEQ_BLOCK
    per_seq = lay.dec_len // SEQ_BLOCK
    vd = DN_HEADS * DN_DK

    def rev(i):
        j = i - n_ctx_blocks
        return jnp.where(i < n_ctx_blocks, i, n_ctx_blocks + (j // per_seq) * per_seq + per_seq - 1 - j % per_seq)

    fwd = lambda i: i
    cols = lambda blk, part: pl.BlockSpec((SEQ_BLOCK, vd), lambda i: (blk(i), part))
    gate_spec = lambda blk: pl.BlockSpec((SEQ_BLOCK, LANES), lambda i: (blk(i), 0))
    state_shape = (2, DN_HEADS, DN_DK, DN_DK)
    s0_spec = pl.BlockSpec((None, None) + state_shape,
                           lambda i: (jnp.maximum(i - n_ctx_blocks, 0) // per_seq, layer_j, 0, 0, 0, 0))
    ctx_states = lambda n: pl.BlockSpec((None, n) + state_shape,
                                        lambda i: (jnp.minimum(i, n_ctx_blocks - 1), 0, 0, 0, 0, 0))
    o_shape = jax.ShapeDtypeStruct((lay.n_tok, vd), F32)
    return pl.pallas_call(
        functools.partial(_delta_kernel, lay=lay, n_prev=n_prev),
        out_shape=(o_shape, o_shape,
                   jax.ShapeDtypeStruct((n_ctx_blocks, n_prev + 1) + state_shape, F32)),
        grid=(n_blocks,),
        in_specs=[cols(fwd, 0), cols(fwd, 1), cols(fwd, 2), gate_spec(fwd),
                  cols(rev, 0), cols(rev, 1), cols(rev, 2), gate_spec(rev), s0_spec]
        + ([ctx_states(n_prev)] if n_prev else []),
        out_specs=(cols(fwd, 0), cols(rev, 0), ctx_states(n_prev + 1)),
        scratch_shapes=[pltpu.VMEM((2 * DN_HEADS, DN_DK, DN_DK), F32)],
        compiler_params=_cparams(1),
        name="delta_rule",
    )(qkv, qkv, qkv, gates, qkv, qkv, qkv, gates, s0, *([prev_states] if n_prev else []))


def _dn_out_kernel(x_ref, of_ref, ob_ref, mod_ref, wz_ref, ng_ref, w_ref, g_ref, b_ref, o_ref, act_ref,
                   *, alpha):
    x = x_ref[...]
    z = jnp.dot(_modulate(x, mod_ref, 1).astype(BF16), wz_ref[...], preferred_element_type=F32)
    for h in range(DN_HEADS):
        lanes = slice(h * DN_DK, (h + 1) * DN_DK)
        o = of_ref[:, lanes] + ob_ref[:, lanes]
        rms = lax.rsqrt(jnp.mean(o * o, axis=-1, keepdims=True) + LN_EPS)
        act_ref[:, lanes] = (o * rms * ng_ref[...] * _silu(z[:, lanes])).astype(BF16)
    y = jnp.dot(act_ref[...], w_ref[...], preferred_element_type=F32)
    o_ref[...] = _residual_norm(x, y, 1.0, mod_ref, 1, g_ref, b_ref, alpha)


def _dn_out(lay, x, o_f, o_b, w_qkvz, z_col_block, mod, layer, norm_g, w_out, g, b, alpha):
    d = lay.d
    vd = DN_HEADS * DN_DK
    return pl.pallas_call(
        functools.partial(_dn_out_kernel, alpha=alpha),
        out_shape=jax.ShapeDtypeStruct((lay.n_tok, d), F32),
        grid=(lay.n_tok // TOK_TILE,),
        in_specs=[_row_spec(TOK_TILE, d), _row_spec(TOK_TILE, vd), _row_spec(TOK_TILE, vd),
                  lay.mod_spec(layer, TOK_TILE), pl.BlockSpec((d, vd), lambda i: (0, z_col_block)),
                  _const_spec((1, DN_DK)), _stacked_spec(w_out, layer // N_MIXERS), _const_spec((1, d)),
                  _const_spec((1, d))],
        out_specs=_row_spec(TOK_TILE, d),
        scratch_shapes=[pltpu.VMEM((TOK_TILE, vd), BF16)],
        compiler_params=_cparams(1),
        name="dn_out",
    )(x, o_f, o_b, mod, w_qkvz, norm_g.reshape(1, DN_DK), w_out, g.reshape(1, d), b.reshape(1, d))


def _dn_mixer(lay, x, mod, layer, w_in, conv_w, a_log, dt_bias, norm_g, w_out, s0, prev_states, g, b, alpha):
    layer_j = layer // N_MIXERS
    qk, vd = DN_HEADS * DN_DK, DN_HEADS * DN_DK
    n_qkv = 2 * qk + vd
    n_gate = 4 * DN_HEADS
    assert w_in.shape[1] == n_qkv + vd + n_gate and n_gate <= LANES
    w_qkvz = w_in[:, :n_qkv + vd].astype(BF16)
    w_ab = jnp.pad(w_in[:, n_qkv + vd:], ((0, 0), (0, LANES - n_gate))).astype(BF16)
    lane_row = lambda v: jnp.pad(v.reshape(1, 2 * DN_HEADS), ((0, 0), (0, LANES - 2 * DN_HEADS)))
    assert conv_w.shape[0] == 3
    conv_pad = jnp.pad(conv_w, ((0, SUBLANES - conv_w.shape[0]), (0, 0)))
    qkv, gates = _dn_in(lay, x, mod, layer, w_qkvz, w_ab, conv_pad, lane_row(a_log), lane_row(dt_bias), n_qkv)
    o_f, o_b, s_fin = _delta_rule(lay, qkv, gates, s0, layer_j, prev_states)
    assert n_qkv % vd == 0
    x = _dn_out(lay, x, o_f, o_b, w_qkvz, n_qkv // vd, mod, layer, norm_g, w_out, g, b, alpha)
    return x, s_fin


def kernel(x_prompt, x_sample, state_delta, c, c_ctx, w_mod, b_mod, ln_g, ln_b, ffn_w_in, ffn_w_out,
           cv_w1, cv_b1, cv_dw, cv_dwb, cv_ln_g, cv_ln_b, cv_w2, cv_b2,
           dn_w_in, dn_conv, dn_a_log, dn_dt_bias, dn_norm_g, dn_w_out):
    n_ctx_seq, ctx_len, d = x_prompt.shape
    n_dec_seq, dec_len, _ = x_sample.shape
    depth = w_mod.shape[0]
    alpha = (2.0 * depth) ** 0.25
    lay = _Layout(n_ctx_seq, ctx_len, n_dec_seq, dec_len, d)

    x = (x_prompt.reshape(lay.n_ctx_tok, d), x_sample.reshape(lay.n_dec_tok, d))
    cond = jnp.concatenate([c_ctx[None, :], c, jnp.zeros((COND_ROWS - 1 - n_dec_seq, d), F32)], axis=0)
    mod = _adaln(cond, w_mod, b_mod).reshape(depth, COND_ROWS, N_MOD, d)

    ffn_w_in, ffn_w_out = ffn_w_in.astype(BF16), ffn_w_out.astype(BF16)
    cv_w1, cv_w2, dn_w_out = cv_w1.astype(BF16), cv_w2.astype(BF16), dn_w_out.astype(BF16)
    ffn_g, ffn_b = ln_g.reshape(depth, 3, 1, d), ln_b.reshape(depth, 3, 1, d)

    states = None
    for l in range(depth):
        x = _ffn(lay, x, mod, l, 0, ffn_w_in, ffn_w_out, ffn_g, ffn_b, alpha)
        j = l // N_MIXERS
        if l % N_MIXERS == 0:
            x = _conv_mixer(lay, x, mod, l, cv_w1, cv_b1[j], cv_dw[j], cv_dwb[j],
                            cv_ln_g[j], cv_ln_b[j], cv_w2, cv_b2[j], ln_g[l, 1], ln_b[l, 1], alpha)
        else:
            x, states = _dn_mixer(lay, x, mod, l, dn_w_in[j], dn_conv[j], dn_a_log[j], dn_dt_bias[j],
                                  dn_norm_g[j], dn_w_out, state_delta, states, ln_g[l, 1], ln_b[l, 1], alpha)
        x = _ffn(lay, x, mod, l, 1, ffn_w_in, ffn_w_out, ffn_g, ffn_b, alpha, split_out=(l == depth - 1))

    y_prompt, y_sample = x
    return (y_prompt.reshape(n_ctx_seq, ctx_len, d), y_sample.reshape(n_dec_seq, dec_len, d),
            states.astype(state_delta.dtype))
```

```python
import functools

import jax
import jax.numpy as jnp
from jax import lax
from jax.experimental import pallas as pl
from jax.experimental.pallas import tpu as pltpu

F32 = jnp.float32
BF16 = jnp.bfloat16

LN_EPS = 1e-5
L2_EPS = 1e-6
N_MOD = 9
N_MIXERS = 2
GRID_W = 64
DN_HEADS = 8
DN_DK = 128
DN_CHUNK = 64
COND_ROWS = 8
LANES = 128
VMEM_LIMIT = 56 * 1024 * 1024

TOK_TILE = 512
FFN_TILE = 1024
SEQ_BLOCK = 256
FF_CHUNK = 256


def _cparams(n_axes):
    return pltpu.CompilerParams(dimension_semantics=("arbitrary",) * n_axes,
                                vmem_limit_bytes=VMEM_LIMIT)


def _mm(a, b):
    return jnp.dot(a.astype(BF16), b.astype(BF16), preferred_element_type=F32)


def _mm_nt(a, b):
    return lax.dot_general(a.astype(BF16), b.astype(BF16), (((1,), (1,)), ((), ())),
                           preferred_element_type=F32)


def _mm_tn(a, b):
    return lax.dot_general(a.astype(BF16), b.astype(BF16), (((0,), (0,)), ((), ())),
                           preferred_element_type=F32)


def _sigmoid(x):
    return 1.0 / (1.0 + jnp.exp(-x))


def _silu(x):
    return x * _sigmoid(x)


def _layer_norm(r, g, b):
    mu = jnp.mean(r, axis=-1, keepdims=True)
    rc = r - mu
    var = jnp.mean(rc * rc, axis=-1, keepdims=True)
    return rc * lax.rsqrt(var + LN_EPS) * g + b


def _modulate(x, mod_ref, slot):
    shift = mod_ref[3 * slot:3 * slot + 1, :]
    scale = mod_ref[3 * slot + 1:3 * slot + 2, :]
    return x * (1.0 + scale) + shift


def _residual_norm(x, y, gate_scale, mod_ref, slot, g_ref, b_ref, alpha):
    gate = mod_ref[3 * slot + 2:3 * slot + 3, :]
    return _layer_norm(alpha * x + (gate_scale * gate) * y, g_ref[...], b_ref[...])


class _Layout:
    def __init__(self, n_ctx_seq, ctx_len, n_dec_seq, dec_len, d_model):
        self.n_ctx_seq, self.ctx_len = n_ctx_seq, ctx_len
        self.n_dec_seq, self.dec_len = n_dec_seq, dec_len
        self.n_ctx_tok = n_ctx_seq * ctx_len
        self.n_dec_tok = n_dec_seq * dec_len
        self.n_tok = self.n_ctx_tok + self.n_dec_tok
        self.d = d_model
        assert n_dec_seq + 1 <= COND_ROWS
        assert self.n_ctx_tok % TOK_TILE == 0 and dec_len % TOK_TILE == 0
        assert ctx_len == SEQ_BLOCK and dec_len % SEQ_BLOCK == 0
        assert dec_len % GRID_W == 0 and SEQ_BLOCK % GRID_W == 0

    def mod_row(self, i, tile):
        n_ctx_tiles = self.n_ctx_tok // tile
        per_seq = self.dec_len // tile
        return jnp.where(i < n_ctx_tiles, 0, 1 + (i - n_ctx_tiles) // per_seq)

    def mod_spec(self, layer, tile):
        return pl.BlockSpec((None, None, N_MOD, self.d),
                            lambda i: (layer, self.mod_row(i, tile), 0, 0))

    def group_specs(self, tile, width):
        n_ctx_tiles = self.n_ctx_tok // tile
        return (pl.BlockSpec((tile, width), lambda i: (jnp.minimum(i, n_ctx_tiles - 1), 0)),
                pl.BlockSpec((tile, width), lambda i: (jnp.maximum(i - n_ctx_tiles, 0), 0)))


def _row_spec(tile, width, col=0):
    return pl.BlockSpec((tile, width), lambda i: (i, col))


def _const_spec(shape):
    return pl.BlockSpec(shape, lambda *_: (0,) * len(shape))


def _stacked_spec(stacked, index):
    shape = stacked.shape[1:]
    return pl.BlockSpec((None,) + shape, lambda *_: (index,) + (0,) * len(shape))


def _adaln_kernel(cond_ref, w_ref, b_ref, o_ref):
    cnd = cond_ref[...]
    o_ref[...] = jnp.dot(_silu(cnd), w_ref[...], preferred_element_type=F32,
                         precision=lax.Precision.HIGHEST) + b_ref[...]


def _adaln(cond, w_mod, b_mod):
    depth, d, n_out = w_mod.shape
    bn = n_out // 4
    return pl.pallas_call(
        _adaln_kernel,
        out_shape=jax.ShapeDtypeStruct((depth, COND_ROWS, n_out), F32),
        grid=(depth, n_out // bn),
        in_specs=[pl.BlockSpec((COND_ROWS, d), lambda l, j: (0, 0)),
                  pl.BlockSpec((None, d, bn), lambda l, j: (l, 0, j)),
                  pl.BlockSpec((None, 1, bn), lambda l, j: (l, 0, j))],
        out_specs=pl.BlockSpec((None, COND_ROWS, bn), lambda l, j: (l, 0, j)),
        compiler_params=_cparams(2),
        name="adaln",
    )(cond, w_mod, b_mod.reshape(depth, 1, n_out))


W_IN_COLS = 256
W_OUT_ROWS = 128


def _stream_cast(src_chunk, stage_ref, sem, n_chunks, store):
    copy = lambda k: pltpu.make_async_copy(src_chunk(k), stage_ref.at[k % 2], sem.at[k % 2])
    copy(0).start()
    for k in range(n_chunks):
        if k + 1 < n_chunks:
            copy(k + 1).start()
        copy(k).wait()
        store(k, stage_ref[k % 2].astype(BF16))


def _ffn_kernel(*refs, layer, sub, d_ff, alpha, n_ctx_tiles, split_in, split_out):
    slot = 2 * sub
    n_in = 2 if split_in else 1
    n_out = 2 if split_out else 1
    x_refs, (mod_ref, w_in_hbm, w_out_hbm, g_ref, b_ref) = refs[:n_in], refs[n_in:n_in + 5]
    o_refs = refs[n_in + 5:n_in + 5 + n_out]
    act_ref, w_in_ref, w_out_ref, stage_in, stage_out, sem = refs[n_in + 5 + n_out:]

    @pl.when(pl.program_id(0) == 0)
    def _():
        w_in_src, w_out_src = w_in_hbm.at[layer, sub], w_out_hbm.at[layer, sub]

        def store_in(k, chunk):
            w_in_ref[:, k * W_IN_COLS:(k + 1) * W_IN_COLS] = chunk

        def store_out(k, chunk):
            w_out_ref[k * W_OUT_ROWS:(k + 1) * W_OUT_ROWS, :] = chunk

        _stream_cast(lambda k: w_in_src.at[:, pl.ds(k * W_IN_COLS, W_IN_COLS)], stage_in, sem,
                     w_in_ref.shape[1] // W_IN_COLS, store_in)
        _stream_cast(lambda k: w_out_src.at[pl.ds(k * W_OUT_ROWS, W_OUT_ROWS), :], stage_out, sem,
                     w_out_ref.shape[0] // W_OUT_ROWS, store_out)

    is_ctx = pl.program_id(0) < n_ctx_tiles
    x = jnp.where(is_ctx, x_refs[0][...], x_refs[1][...]) if split_in else x_refs[0][...]
    h = _modulate(x, mod_ref, slot).astype(BF16)
    for j in range(d_ff // FF_CHUNK):
        lo = j * FF_CHUNK
        lin = jnp.dot(h, w_in_ref[:, lo:lo + FF_CHUNK], preferred_element_type=F32)
        gat = jnp.dot(h, w_in_ref[:, d_ff + lo:d_ff + lo + FF_CHUNK], preferred_element_type=F32)
        act_ref[:, lo:lo + FF_CHUNK] = (_silu(gat) * lin).astype(BF16)
    y = jnp.dot(act_ref[...], w_out_ref[...], preferred_element_type=F32)
    out = _residual_norm(x, y, 0.5, mod_ref, slot, g_ref, b_ref, alpha)
    if not split_out:
        o_refs[0][...] = out
    else:
        @pl.when(is_ctx)
        def _():
            o_refs[0][...] = out

        @pl.when(jnp.logical_not(is_ctx))
        def _():
            o_refs[1][...] = out


def _ffn(lay, xs, mod, layer, sub, w_in, w_out, g, b, alpha, split_out=False):
    d, d_ff = lay.d, w_out.shape[2]
    assert d_ff % FF_CHUNK == 0 and (2 * d_ff) % W_IN_COLS == 0 and d_ff % W_OUT_ROWS == 0
    assert w_in.dtype == F32 and w_out.dtype == F32
    split_in = isinstance(xs, tuple)
    tile = FFN_TILE
    assert lay.n_ctx_tok % tile == 0 and lay.dec_len % tile == 0
    x_specs = list(lay.group_specs(tile, d)) if split_in else [_row_spec(tile, d)]
    if split_out:
        out_shape = (jax.ShapeDtypeStruct((lay.n_ctx_tok, d), F32), jax.ShapeDtypeStruct((lay.n_dec_tok, d), F32))
        out_specs = lay.group_specs(tile, d)
    else:
        out_shape = jax.ShapeDtypeStruct((lay.n_tok, d), F32)
        out_specs = _row_spec(tile, d)
    return pl.pallas_call(
        functools.partial(_ffn_kernel, layer=layer, sub=sub, d_ff=d_ff, alpha=alpha,
                          n_ctx_tiles=lay.n_ctx_tok // tile, split_in=split_in, split_out=split_out),
        out_shape=out_shape,
        grid=(lay.n_tok // tile,),
        in_specs=x_specs + [lay.mod_spec(layer, tile),
                            pl.BlockSpec(memory_space=pl.ANY), pl.BlockSpec(memory_space=pl.ANY),
                            pl.BlockSpec((None, None, 1, d), lambda i: (layer, 2 * sub, 0, 0)),
                            pl.BlockSpec((None, None, 1, d), lambda i: (layer, 2 * sub, 0, 0))],
        out_specs=out_specs,
        scratch_shapes=[pltpu.VMEM((tile, d_ff), BF16),
                        pltpu.VMEM((d, 2 * d_ff), BF16), pltpu.VMEM((d_ff, d), BF16),
                        pltpu.VMEM((2, d, W_IN_COLS), F32), pltpu.VMEM((2, W_OUT_ROWS, d), F32),
                        pltpu.SemaphoreType.DMA((2,))],
        compiler_params=_cparams(1),
        name="ffn",
    )(*(xs if split_in else (xs,)), mod, w_in, w_out, g, b)


def _conv_in_kernel(x_ref, mod_ref, w1_ref, b1_ref, u_ref, *, d):
    h = _modulate(x_ref[...], mod_ref, 1).astype(BF16)
    lin = jnp.dot(h, w1_ref[:, :d], preferred_element_type=F32) + b1_ref[:, :d]
    gat = jnp.dot(h, w1_ref[:, d:], preferred_element_type=F32) + b1_ref[:, d:]
    u_ref[...] = lin * _sigmoid(gat)


def _conv_in(lay, x, mod, layer, w1, b1):
    d = lay.d
    return pl.pallas_call(
        functools.partial(_conv_in_kernel, d=d),
        out_shape=jax.ShapeDtypeStruct((lay.n_tok, d), F32),
        grid=(lay.n_tok // TOK_TILE,),
        in_specs=[_row_spec(TOK_TILE, d), lay.mod_spec(layer, TOK_TILE),
                  _stacked_spec(w1, layer // N_MIXERS), _const_spec((1, 2 * d))],
        out_specs=_row_spec(TOK_TILE, d),
        compiler_params=_cparams(1),
        name="conv_in",
    )(x, mod, w1, b1.reshape(1, 2 * d))


CONV_PAD = 16
CONV_ROWS = 64


SUBLANES = 8
DEC_CONV_CH = 256


def _segment_conv(u_ref, w_ref, o_ref, pad_ref, sh_ref, *, taps, seg_len, n_seg):
    n_ch = u_ref.shape[1]
    half = taps // 2
    n_sh = sh_ref.shape[1]
    zeros = jnp.zeros((CONV_PAD, n_ch), F32)
    pad_ref[0:CONV_PAD, :] = zeros
    pad_ref[CONV_PAD + seg_len:2 * CONV_PAD + seg_len, :] = zeros

    def segment(s, carry):
        row0 = pl.multiple_of(s * seg_len, seg_len)
        pad_ref[CONV_PAD:CONV_PAD + seg_len, :] = u_ref[pl.ds(row0, seg_len), :]

        def lane_tile(j, carry):
            lanes = pl.ds(pl.multiple_of(j * LANES, LANES), LANES)
            for p in range(1, SUBLANES):
                sh_ref[p] = pad_ref[p:p + n_sh, lanes]
            for r in range(seg_len // CONV_ROWS):
                acc = jnp.zeros((CONV_ROWS, LANES), F32)
                for k in range(taps):
                    off = CONV_PAD - half + k
                    p = off % SUBLANES
                    start = off - p + r * CONV_ROWS
                    src = (pad_ref[start:start + CONV_ROWS, lanes] if p == 0
                           else sh_ref[p, start:start + CONV_ROWS, :])
                    acc = acc + w_ref[k:k + 1, lanes] * src
                o_ref[pl.ds(row0 + r * CONV_ROWS, CONV_ROWS), lanes] = acc
            return carry

        return lax.fori_loop(0, n_ch // LANES, lane_tile, carry)

    lax.fori_loop(0, n_seg, segment, 0)


def _segment_conv_scratch(seg_len, n_ch):
    return [pltpu.VMEM((seg_len + 2 * CONV_PAD, n_ch), F32),
            pltpu.VMEM((SUBLANES, seg_len + 2 * CONV_PAD - SUBLANES, LANES), F32)]


def _ctx_conv_kernel(u_ref, w_ref, o_ref, pad_ref, sh_ref, *, taps):
    _segment_conv(u_ref, w_ref, o_ref, pad_ref, sh_ref, taps=taps, seg_len=u_ref.shape[0], n_seg=1)


def _ctx_conv(lay, u, w, taps):
    d = lay.d
    return pl.pallas_call(
        functools.partial(_ctx_conv_kernel, taps=taps),
        out_shape=jax.ShapeDtypeStruct((lay.n_ctx_tok, d), F32),
        grid=(lay.n_ctx_seq,),
        in_specs=[pl.BlockSpec((lay.ctx_len, d), lambda i: (i, 0)),
                  pl.BlockSpec((w.shape[0], d), lambda i: (0, 0))],
        out_specs=pl.BlockSpec((lay.ctx_len, d), lambda i: (i, 0)),
        scratch_shapes=_segment_conv_scratch(lay.ctx_len, d),
        compiler_params=_cparams(1),
        name="ctx_conv",
    )(u, w)


def _dec_conv_kernel(u_ref, w_ref, o_ref, pad_ref, sh_ref, vpad_ref, *, taps, n_row_blocks):
    j = pl.program_id(1)
    seq_len, n_ch = u_ref.shape

    @pl.when(j < n_row_blocks)
    def _():
        _segment_conv(u_ref, w_ref, o_ref, pad_ref, sh_ref, taps=taps, seg_len=GRID_W,
                      n_seg=seq_len // GRID_W)

    @pl.when(j >= n_row_blocks)
    def _():
        halo = (taps // 2) * GRID_W
        zeros = jnp.zeros((halo, n_ch), F32)
        vpad_ref[0:halo, :] = zeros
        vpad_ref[halo + seq_len:2 * halo + seq_len, :] = zeros
        vpad_ref[halo:halo + seq_len, :] = u_ref[...]

        def row_chunk(r, carry):
            base = pl.multiple_of(r * CONV_ROWS, CONV_ROWS)
            for t in range(n_ch // LANES):
                lanes = slice(t * LANES, (t + 1) * LANES)
                acc = jnp.zeros((CONV_ROWS, LANES), F32)
                for k in range(taps):
                    acc = acc + w_ref[k:k + 1, lanes] * vpad_ref[pl.ds(base + k * GRID_W, CONV_ROWS), lanes]
                o_ref[pl.ds(base, CONV_ROWS), lanes] = acc
            return carry

        lax.fori_loop(0, seq_len // CONV_ROWS, row_chunk, 0)


def _dec_conv(lay, u, w, taps):
    d = lay.d
    first_seq = lay.n_ctx_tok // lay.dec_len
    assert lay.n_ctx_tok % lay.dec_len == 0 and (d // 2) % DEC_CONV_CH == 0
    halo = (taps // 2) * GRID_W
    return pl.pallas_call(
        functools.partial(_dec_conv_kernel, taps=taps, n_row_blocks=(d // 2) // DEC_CONV_CH),
        out_shape=jax.ShapeDtypeStruct((lay.n_dec_tok, d), F32),
        grid=(lay.n_dec_seq, d // DEC_CONV_CH),
        in_specs=[pl.BlockSpec((lay.dec_len, DEC_CONV_CH), lambda b, j: (first_seq + b, j)),
                  pl.BlockSpec((w.shape[0], DEC_CONV_CH), lambda b, j: (0, j))],
        out_specs=pl.BlockSpec((lay.dec_len, DEC_CONV_CH), lambda b, j: (b, j)),
        scratch_shapes=_segment_conv_scratch(GRID_W, DEC_CONV_CH)
        + [pltpu.VMEM((lay.dec_len + 2 * halo, DEC_CONV_CH), F32)],
        compiler_params=_cparams(2),
        name="dec_conv",
    )(u, w)


def _conv_out_kernel(x_ref, cvc_ref, cvd_ref, mod_ref, dwb_ref, lng_ref, lnb_ref, w2_ref, b2_ref, g_ref, b_ref,
                     o_ref, *, alpha, n_ctx_tiles):
    cv = jnp.where(pl.program_id(0) < n_ctx_tiles, cvc_ref[...], cvd_ref[...])
    a = _silu(_layer_norm(cv + dwb_ref[...], lng_ref[...], lnb_ref[...]))
    y = jnp.dot(a.astype(BF16), w2_ref[...], preferred_element_type=F32) + b2_ref[...]
    o_ref[...] = _residual_norm(x_ref[...], y, 1.0, mod_ref, 1, g_ref, b_ref, alpha)


def _conv_out(lay, x, cv_ctx, cv_dec, mod, layer, dwb, lng, lnb, w2, b2, g, b, alpha):
    d = lay.d
    vec = lambda v: v.reshape(1, d)
    return pl.pallas_call(
        functools.partial(_conv_out_kernel, alpha=alpha, n_ctx_tiles=lay.n_ctx_tok // TOK_TILE),
        out_shape=jax.ShapeDtypeStruct((lay.n_tok, d), F32),
        grid=(lay.n_tok // TOK_TILE,),
        in_specs=[_row_spec(TOK_TILE, d), *lay.group_specs(TOK_TILE, d), lay.mod_spec(layer, TOK_TILE),
                  _const_spec((1, d)), _const_spec((1, d)), _const_spec((1, d)),
                  _stacked_spec(w2, layer // N_MIXERS), _const_spec((1, d)), _const_spec((1, d)),
                  _const_spec((1, d))],
        out_specs=_row_spec(TOK_TILE, d),
        compiler_params=_cparams(1),
        name="conv_out",
    )(x, cv_ctx, cv_dec, mod, vec(dwb), vec(lng), vec(lnb), w2, vec(b2), vec(g), vec(b))


def _conv_mixer(lay, x, mod, layer, w1, b1, dw, dwb, lng, lnb, w2, b2, g, b, alpha):
    taps = dw.shape[0]
    assert taps // 2 < CONV_PAD
    u = _conv_in(lay, x, mod, layer, w1, b1)
    dw_pad = jnp.pad(dw, ((0, -taps % SUBLANES), (0, 0)))
    cv_ctx = _ctx_conv(lay, u, dw_pad, taps)
    cv_dec = _dec_conv(lay, u, dw_pad, taps)
    return _conv_out(lay, x, cv_ctx, cv_dec, mod, layer, dwb, lng, lnb, w2, b2, g, b, alpha)


def _split3(x):
    hi = x.astype(BF16)
    r1 = x - hi.astype(F32)
    mid = r1.astype(BF16)
    lo = (r1 - mid.astype(F32)).astype(BF16)
    return hi, mid, lo


def _chunk_cumsums(tri, x):
    n = x.shape[1]
    parts = jnp.dot(tri, jnp.concatenate(_split3(x), axis=1), preferred_element_type=F32)
    sums = parts[:, :n] + parts[:, n:2 * n] + parts[:, 2 * n:]
    return sums[:x.shape[0]], sums[x.shape[0]:]


DN_HALO = 16
DN_COLS = 256


def _dn_in_kernel(x_ref, xp_ref, xn_ref, mod_ref, w_ref, wab_ref, cw_ref, alog_ref, dtb_ref,
                  qkv_ref, gates_ref, h_ref, tri_ref, *, lay, n_qkv):
    i = pl.program_id(0)
    n_ctx_blocks = lay.n_ctx_tok // SEQ_BLOCK
    per_seq = lay.dec_len // SEQ_BLOCK
    pos = (i - n_ctx_blocks) % per_seq
    is_ctx = i < n_ctx_blocks
    has_prev = jnp.logical_not(is_ctx | (pos == 0))
    has_next = jnp.logical_not(is_ctx | (pos == per_seq - 1))
    lo, hi = DN_HALO, DN_HALO + SEQ_BLOCK
    h_ref[0:lo] = jnp.where(has_prev, _modulate(xp_ref[...], mod_ref, 1), 0.0).astype(BF16)
    h_ref[lo:hi] = _modulate(x_ref[...], mod_ref, 1).astype(BF16)
    h_ref[hi:hi + DN_HALO] = jnp.where(has_next, _modulate(xn_ref[...], mod_ref, 1), 0.0).astype(BF16)

    n_qk = 2 * DN_HEADS * DN_DK
    n_rows = h_ref.shape[0]
    for j in range(n_qkv // DN_COLS):
        cols = slice(j * DN_COLS, (j + 1) * DN_COLS)
        pre = jnp.dot(h_ref[...], w_ref[:, cols], preferred_element_type=F32)
        y = _silu(cw_ref[0:1, cols] * pltpu.roll(pre, 1, 0)[lo:hi] + cw_ref[1:2, cols] * pre[lo:hi]
                  + cw_ref[2:3, cols] * pltpu.roll(pre, n_rows - 1, 0)[lo:hi])
        for t in range(DN_COLS // DN_DK):
            c0 = j * DN_COLS + t * DN_DK
            yt = y[:, t * DN_DK:(t + 1) * DN_DK]
            if c0 < n_qk:
                inv = lax.rsqrt(jnp.sum(yt * yt, axis=-1, keepdims=True) + L2_EPS)
                yt = yt * (inv * (DN_DK ** -0.5 if c0 < n_qk // 2 else 1.0))
            qkv_ref[:, c0:c0 + DN_DK] = yt

    ab = jnp.dot(h_ref[lo:hi], wab_ref[...], preferred_element_type=F32)
    pre = ab + dtb_ref[...]
    softplus = jnp.maximum(pre, 0.0) + jnp.log(1.0 + jnp.exp(-jnp.abs(pre)))
    log_a = -jnp.exp(alog_ref[...]) * softplus
    beta = _sigmoid(ab)

    @pl.when(i == 0)
    def _():
        tile = ab.shape[0]
        row = lax.broadcasted_iota(jnp.int32, (tile, tile), 0)
        col = lax.broadcasted_iota(jnp.int32, (tile, tile), 1)
        same = (row // DN_CHUNK) == (col // DN_CHUNK)
        tri_ref[0:tile] = (same & (row >= col)).astype(BF16)
        tri_ref[tile:2 * tile] = (same & (row <= col)).astype(BF16)

    lane = lax.broadcasted_iota(jnp.int32, ab.shape, 1)
    g_fwd, g_bwd = _chunk_cumsums(tri_ref[...], log_a)
    gates_ref[...] = jnp.where(lane < DN_HEADS, g_fwd, jnp.where(lane < 2 * DN_HEADS, g_bwd, beta))


def _dn_in(lay, x, mod, layer, w_qkvz, w_ab, conv_w, alog_row, dtb_row, n_qkv):
    d = w_qkvz.shape[0]
    assert n_qkv % DN_COLS == 0 and SEQ_BLOCK % DN_HALO == 0
    sub = SEQ_BLOCK // DN_HALO
    n_units = lay.n_tok // DN_HALO
    return pl.pallas_call(
        functools.partial(_dn_in_kernel, lay=lay, n_qkv=n_qkv),
        out_shape=(jax.ShapeDtypeStruct((lay.n_tok, n_qkv), F32),
                   jax.ShapeDtypeStruct((lay.n_tok, LANES), F32)),
        grid=(lay.n_tok // SEQ_BLOCK,),
        in_specs=[_row_spec(SEQ_BLOCK, d),
                  pl.BlockSpec((DN_HALO, d), lambda i: (jnp.maximum(i * sub - 1, 0), 0)),
                  pl.BlockSpec((DN_HALO, d), lambda i: (jnp.minimum((i + 1) * sub, n_units - 1), 0)),
                  lay.mod_spec(layer, SEQ_BLOCK),
                  _const_spec((d, n_qkv)), _const_spec((d, LANES)), _const_spec((SUBLANES, n_qkv)),
                  _const_spec((1, LANES)), _const_spec((1, LANES))],
        out_specs=(_row_spec(SEQ_BLOCK, n_qkv), _row_spec(SEQ_BLOCK, LANES)),
        scratch_shapes=[pltpu.VMEM((SEQ_BLOCK + 2 * DN_HALO, d), BF16),
                        pltpu.VMEM((2 * SEQ_BLOCK, SEQ_BLOCK), BF16)],
        compiler_params=_cparams(1),
        name="dn_in",
    )(x, x, x, mod, w_qkvz, w_ab, conv_w, alog_row, dtb_row)


def _bmm(a, b):
    return jnp.einsum("bmk,bkn->bmn", a.astype(BF16), b.astype(BF16), preferred_element_type=F32)


def _bmm_nt(a, b):
    return jnp.einsum("bmk,bnk->bmn", a.astype(BF16), b.astype(BF16), preferred_element_type=F32)


def _bmm_tn(a, b):
    return jnp.einsum("bkm,bkn->bmn", a.astype(BF16), b.astype(BF16), preferred_element_type=F32)


def _chunk_masks():
    c = DN_CHUNK
    row = lax.broadcasted_iota(jnp.int32, (c, 2 * c), 0)
    col = lax.broadcasted_iota(jnp.int32, (c, 2 * c), 1) % c
    same = lambda n: (row // n) == (col // n)
    levels = []
    n = 8
    while n < c:
        levels.append(same(2 * n) & jnp.logical_not(same(n)))
        n *= 2
    return row, col, same(8), levels


def _block_diag(y):
    c = y.shape[1]
    lane = lax.broadcasted_iota(jnp.int32, y.shape[1:], 1)[None]
    zero = jnp.zeros((), y.dtype)
    return jnp.concatenate([jnp.where(lane < c, y, zero), jnp.where(lane >= c, y, zero)], axis=1)


def _pair_rows(x0, x1):
    zeros = jnp.zeros_like(x0)
    return jnp.concatenate([jnp.concatenate([x0, zeros], axis=2),
                            jnp.concatenate([zeros, x1], axis=2)], axis=1)


def _dir_where(mask_fwd, mask_bwd, x, other):
    half = x.shape[0] // 2
    return jnp.concatenate([jnp.where(mask_fwd[None], x[:half], other),
                            jnp.where(mask_bwd[None], x[half:], other)], axis=0)


def _unit_tri_inverse_minus_eye(a, base_mask, level_masks):
    bd = lambda y: _block_diag(y.astype(BF16))
    d = jnp.where(base_mask[None], a, 0.0)
    c = a.shape[1]
    x = _bmm(d, bd(d))
    xx_dx = _bmm(jnp.concatenate([x, d], axis=1), bd(x))
    x2 = xx_dx[:, :c]
    m1 = x - d - xx_dx[:, c:]
    n = m1 + x2 + _bmm(m1, bd(x2))
    for mask in level_masks:
        e = jnp.where(mask[None], a, 0.0)
        y = e + _bmm(n, bd(e))
        n = n - y - _bmm(y, bd(n))
    return n


def _lane_broadcast(x, lane):
    lanes = lax.broadcasted_iota(jnp.int32, x.shape, 1)
    col = jnp.sum(jnp.where(lanes == lane, x, 0.0), axis=-1, keepdims=True)
    return jnp.broadcast_to(col, x.shape)


def _delta_kernel(*refs, lay, n_prev):
    if n_prev:
        (qf_ref, kf_ref, vf_ref, gf_ref, qb_ref, kb_ref, vb_ref, gb_ref, s0_ref, prev_ref,
         of_ref, ob_ref, sout_ref, s_ref) = refs
    else:
        (qf_ref, kf_ref, vf_ref, gf_ref, qb_ref, kb_ref, vb_ref, gb_ref, s0_ref,
         of_ref, ob_ref, sout_ref, s_ref) = refs
    i = pl.program_id(0)
    n_ctx_blocks = lay.n_ctx_tok // SEQ_BLOCK
    per_seq = lay.dec_len // SEQ_BLOCK
    nc, nh, c = SEQ_BLOCK // DN_CHUNK, DN_HEADS, DN_CHUNK
    npair = nh // 2
    assert nh % 2 == 0 and 2 * c == LANES and DN_DK == LANES
    heads = [2 * m + parity for parity in range(2) for m in range(npair)]
    is_ctx = i < n_ctx_blocks
    seq_start = is_ctx | ((i - n_ctx_blocks) % per_seq == 0)

    @pl.when(seq_start)
    def _():
        s0 = jnp.stack([s0_ref[d, h] for d in range(2) for h in heads])
        s_ref[...] = jnp.where(is_ctx, 0.0, s0)

    views = ((qf_ref, kf_ref, vf_ref, gf_ref), (qb_ref, kb_ref, vb_ref, gb_ref))
    lane = lax.broadcasted_iota(jnp.int32, (c, LANES), 1)
    qs, ks, vs, gs, bs = [], [], [], [], []
    q2s, k2s, g2s, b2s, gr2s = [], [], [], [], []
    for d, (q_ref, k_ref, v_ref, g_ref) in enumerate(views):
        gates_all = g_ref[...]
        gates_t = jnp.transpose(gates_all)
        for ci in range(nc):
            rows = slice(ci * c, (ci + 1) * c)
            gates = gates_all[rows]
            g_h = [_lane_broadcast(gates, d * nh + h) for h in range(nh)]
            b_h = [_lane_broadcast(gates, (2 + d) * nh + h) for h in range(nh)]
            for h in heads:
                lanes = slice(h * DN_DK, (h + 1) * DN_DK)
                qs.append(q_ref[rows, lanes])
                ks.append(k_ref[rows, lanes])
                vs.append(v_ref[rows, lanes])
                gs.append(g_h[h])
                bs.append(b_h[h])
            for m in range(npair):
                lanes2 = slice(2 * m * DN_DK, (2 * m + 2) * DN_DK)
                q2s.append(q_ref[rows, lanes2])
                k2s.append(k_ref[rows, lanes2])
                g2s.append(jnp.where(lane < c, g_h[2 * m], g_h[2 * m + 1]))
                b2s.append(jnp.where(lane < c, b_h[2 * m], b_h[2 * m + 1]))
                l0 = d * nh + 2 * m
                gr2s.append(jnp.concatenate([gates_t[l0:l0 + 1, rows], gates_t[l0 + 1:l0 + 2, rows]], axis=1))
    q, k, v = jnp.stack(qs), jnp.stack(ks), jnp.stack(vs)
    g, beta = jnp.stack(gs), jnp.stack(bs)
    q2, k2 = jnp.stack(q2s), jnp.stack(k2s)
    g2, beta2, g_row2 = jnp.stack(g2s), jnp.stack(b2s), jnp.stack(gr2s)
    half, half2 = nc * nh, nc * npair
    n_groups = 2 * nc

    row, col, base_mask, level_masks = _chunk_masks()
    lane2 = lax.broadcasted_iota(jnp.int32, (c, 2 * DN_DK), 1)[None]
    k_bd = jnp.concatenate([jnp.where(lane2 < DN_DK, k2, 0.0), jnp.where(lane2 >= DN_DK, k2, 0.0)], axis=1)
    qk_kk = _bmm_nt(jnp.concatenate([q2, k2], axis=1), k_bd)
    decay = jnp.exp(_dir_where(row >= col, row <= col, g2 - g_row2, -jnp.inf))
    a = beta2 * qk_kk[:, c:] * _dir_where(row > col, row < col, decay, 0.0)
    p = qk_kk[:, :c] * decay
    n = _unit_tri_inverse_minus_eye(a, base_mask, level_masks)
    eg = jnp.exp(g)
    g_last = jnp.concatenate([g[:half, c - 1:c], g[half:, 0:1]], axis=0)
    rhs = jnp.concatenate([beta * v, (beta * eg) * k], axis=2)
    width = rhs.shape[2]
    parity_slots = lambda x, par: jnp.concatenate(
        [x[grp * nh + par * npair:grp * nh + (par + 1) * npair] for grp in range(n_groups)], axis=0)
    n_rhs = _bmm(n, _pair_rows(parity_slots(rhs, 0), parity_slots(rhs, 1)))
    sol = rhs + jnp.concatenate(
        [n_rhs[grp * npair:(grp + 1) * npair, :, par * width:(par + 1) * width]
         for grp in range(n_groups) for par in range(2)], axis=0)
    u_t = sol[:, :, :DN_DK]
    wq = jnp.concatenate([sol[:, :, DN_DK:], q * eg], axis=1)
    kd = k * jnp.exp(g_last - g)
    gl = jnp.exp(g_last)

    s = s_ref[...]
    for ci in range(nc):
        cb = nc - 1 - ci
        pick = lambda x: jnp.concatenate([x[ci * nh:(ci + 1) * nh],
                                          x[half + cb * nh:half + (cb + 1) * nh]], axis=0)
        p_now = jnp.concatenate([p[ci * npair:(ci + 1) * npair],
                                 p[half2 + cb * npair:half2 + (cb + 1) * npair]], axis=0)
        ws = _bmm(pick(wq), s)
        u = pick(u_t) - ws[:, :c]
        u_bd = jnp.concatenate([_pair_rows(u[0:npair], u[npair:nh]),
                                _pair_rows(u[nh:nh + npair], u[nh + npair:])], axis=0)
        pu = _bmm(p_now, u_bd)
        o = ws[:, c:] + jnp.concatenate([pu[:npair, :, :DN_DK], pu[:npair, :, DN_DK:],
                                         pu[npair:, :, :DN_DK], pu[npair:, :, DN_DK:]], axis=0)
        s = pick(gl) * s + _bmm_tn(pick(kd), u)
        for slot, h in enumerate(heads):
            lanes = slice(h * DN_DK, (h + 1) * DN_DK)
            of_ref[ci * c:(ci + 1) * c, lanes] = o[slot]
            ob_ref[cb * c:(cb + 1) * c, lanes] = o[nh + slot]
    s_ref[...] = s

    @pl.when(is_ctx)
    def _():
        if n_prev:
            sout_ref[0:n_prev] = prev_ref[...]
        for d in range(2):
            for slot, h in enumerate(heads):
                sout_ref[n_prev, d, h] = s[d * nh + slot]


def _delta_rule(lay, qkv, gates, s0, layer_j, prev_states):
    n_prev = 0 if prev_states is None else prev_states.shape[1]
    assert n_prev == layer_j
    n_blocks = lay.n_tok // SEQ_BLOCK
    n_ctx_blocks = lay.n_ctx_tok // SEQ_BLOCK
    per_seq = lay.dec_len // SEQ_BLOCK
    vd = DN_HEADS * DN_DK

    def rev(i):
        j = i - n_ctx_blocks
        return jnp.where(i < n_ctx_blocks, i, n_ctx_blocks + (j // per_seq) * per_seq + per_seq - 1 - j % per_seq)

    fwd = lambda i: i
    cols = lambda blk, part: pl.BlockSpec((SEQ_BLOCK, vd), lambda i: (blk(i), part))
    gate_spec = lambda blk: pl.BlockSpec((SEQ_BLOCK, LANES), lambda i: (blk(i), 0))
    state_shape = (2, DN_HEADS, DN_DK, DN_DK)
    s0_spec = pl.BlockSpec((None, None) + state_shape,
                           lambda i: (jnp.maximum(i - n_ctx_blocks, 0) // per_seq, layer_j, 0, 0, 0, 0))
    ctx_states = lambda n: pl.BlockSpec((None, n) + state_shape,
                                        lambda i: (jnp.minimum(i, n_ctx_blocks - 1), 0, 0, 0, 0, 0))
    o_shape = jax.ShapeDtypeStruct((lay.n_tok, vd), F32)
    return pl.pallas_call(
        functools.partial(_delta_kernel, lay=lay, n_prev=n_prev),
        out_shape=(o_shape, o_shape,
                   jax.ShapeDtypeStruct((n_ctx_blocks, n_prev + 1) + state_shape, F32)),
        grid=(n_blocks,),
        in_specs=[cols(fwd, 0), cols(fwd, 1), cols(fwd, 2), gate_spec(fwd),
                  cols(rev, 0), cols(rev, 1), cols(rev, 2), gate_spec(rev), s0_spec]
        + ([ctx_states(n_prev)] if n_prev else []),
        out_specs=(cols(fwd, 0), cols(rev, 0), ctx_states(n_prev + 1)),
        scratch_shapes=[pltpu.VMEM((2 * DN_HEADS, DN_DK, DN_DK), F32)],
        compiler_params=_cparams(1),
        name="delta_rule",
    )(qkv, qkv, qkv, gates, qkv, qkv, qkv, gates, s0, *([prev_states] if n_prev else []))


def _dn_out_kernel(x_ref, of_ref, ob_ref, mod_ref, wz_ref, ng_ref, w_ref, g_ref, b_ref, o_ref, act_ref,
                   *, alpha):
    x = x_ref[...]
    z = jnp.dot(_modulate(x, mod_ref, 1).astype(BF16), wz_ref[...], preferred_element_type=F32)
    for h in range(DN_HEADS):
        lanes = slice(h * DN_DK, (h + 1) * DN_DK)
        o = of_ref[:, lanes] + ob_ref[:, lanes]
        rms = lax.rsqrt(jnp.mean(o * o, axis=-1, keepdims=True) + LN_EPS)
        act_ref[:, lanes] = (o * rms * ng_ref[...] * _silu(z[:, lanes])).astype(BF16)
    y = jnp.dot(act_ref[...], w_ref[...], preferred_element_type=F32)
    o_ref[...] = _residual_norm(x, y, 1.0, mod_ref, 1, g_ref, b_ref, alpha)


def _dn_out(lay, x, o_f, o_b, w_qkvz, z_col_block, mod, layer, norm_g, w_out, g, b, alpha):
    d = lay.d
    vd = DN_HEADS * DN_DK
    return pl.pallas_call(
        functools.partial(_dn_out_kernel, alpha=alpha),
        out_shape=jax.ShapeDtypeStruct((lay.n_tok, d), F32),
        grid=(lay.n_tok // TOK_TILE,),
        in_specs=[_row_spec(TOK_TILE, d), _row_spec(TOK_TILE, vd), _row_spec(TOK_TILE, vd),
                  lay.mod_spec(layer, TOK_TILE), pl.BlockSpec((d, vd), lambda i: (0, z_col_block)),
                  _const_spec((1, DN_DK)), _stacked_spec(w_out, layer // N_MIXERS), _const_spec((1, d)),
                  _const_spec((1, d))],
        out_specs=_row_spec(TOK_TILE, d),
        scratch_shapes=[pltpu.VMEM((TOK_TILE, vd), BF16)],
        compiler_params=_cparams(1),
        name="dn_out",
    )(x, o_f, o_b, mod, w_qkvz, norm_g.reshape(1, DN_DK), w_out, g.reshape(1, d), b.reshape(1, d))


def _dn_mixer(lay, x, mod, layer, w_in, conv_w, a_log, dt_bias, norm_g, w_out, s0, prev_states, g, b, alpha):
    layer_j = layer // N_MIXERS
    qk, vd = DN_HEADS * DN_DK, DN_HEADS * DN_DK
    n_qkv = 2 * qk + vd
    n_gate = 4 * DN_HEADS
    assert w_in.shape[1] == n_qkv + vd + n_gate and n_gate <= LANES
    w_qkvz = w_in[:, :n_qkv + vd].astype(BF16)
    w_ab = jnp.pad(w_in[:, n_qkv + vd:], ((0, 0), (0, LANES - n_gate))).astype(BF16)
    lane_row = lambda v: jnp.pad(v.reshape(1, 2 * DN_HEADS), ((0, 0), (0, LANES - 2 * DN_HEADS)))
    assert conv_w.shape[0] == 3
    conv_pad = jnp.pad(conv_w, ((0, SUBLANES - conv_w.shape[0]), (0, 0)))
    qkv, gates = _dn_in(lay, x, mod, layer, w_qkvz, w_ab, conv_pad, lane_row(a_log), lane_row(dt_bias), n_qkv)
    o_f, o_b, s_fin = _delta_rule(lay, qkv, gates, s0, layer_j, prev_states)
    assert n_qkv % vd == 0
    x = _dn_out(lay, x, o_f, o_b, w_qkvz, n_qkv // vd, mod, layer, norm_g, w_out, g, b, alpha)
    return x, s_fin


def kernel(x_prompt, x_sample, state_delta, c, c_ctx, w_mod, b_mod, ln_g, ln_b, ffn_w_in, ffn_w_out,
           cv_w1, cv_b1, cv_dw, cv_dwb, cv_ln_g, cv_ln_b, cv_w2, cv_b2,
           dn_w_in, dn_conv, dn_a_log, dn_dt_bias, dn_norm_g, dn_w_out):
    n_ctx_seq, ctx_len, d = x_prompt.shape
    n_dec_seq, dec_len, _ = x_sample.shape
    depth = w_mod.shape[0]
    alpha = (2.0 * depth) ** 0.25
    lay = _Layout(n_ctx_seq, ctx_len, n_dec_seq, dec_len, d)

    x = (x_prompt.reshape(lay.n_ctx_tok, d), x_sample.reshape(lay.n_dec_tok, d))
    cond = jnp.concatenate([c_ctx[None, :], c, jnp.zeros((COND_ROWS - 1 - n_dec_seq, d), F32)], axis=0)
    mod = _adaln(cond, w_mod, b_mod).reshape(depth, COND_ROWS, N_MOD, d)

    cv_w1, cv_w2, dn_w_out = cv_w1.astype(BF16), cv_w2.astype(BF16), dn_w_out.astype(BF16)
    ffn_g, ffn_b = ln_g.reshape(depth, 3, 1, d), ln_b.reshape(depth, 3, 1, d)

    states = None
    for l in range(depth):
        x = _ffn(lay, x, mod, l, 0, ffn_w_in, ffn_w_out, ffn_g, ffn_b, alpha)
        j = l // N_MIXERS
        if l % N_MIXERS == 0:
            x = _conv_mixer(lay, x, mod, l, cv_w1, cv_b1[j], cv_dw[j], cv_dwb[j],
                            cv_ln_g[j], cv_ln_b[j], cv_w2, cv_b2[j], ln_g[l, 1], ln_b[l, 1], alpha)
        else:
            x, states = _dn_mixer(lay, x, mod, l, dn_w_in[j], dn_conv[j], dn_a_log[j], dn_dt_bias[j],
                                  dn_norm_g[j], dn_w_out, state_delta, states, ln_g[l, 1], ln_b[l, 1], alpha)
        x = _ffn(lay, x, mod, l, 1, ffn_w_in, ffn_w_out, ffn_g, ffn_b, alpha, split_out=(l == depth - 1))

    y_prompt, y_sample = x
    return (y_prompt.reshape(n_ctx_seq, ctx_len, d), y_sample.reshape(n_dec_seq, dec_len, d),
            states.astype(state_delta.dtype))
```
